```python
import jax, jax.numpy as jnp
from jax import lax
import numpy as np

D_MODEL = 4096
BATCH = 4
SEQ = 4096
DEPTH = 4

MIX_WIDTH = D_MODEL // 4
FOX_HEAD_DIM = 128
FOX_HEADS = MIX_WIDTH // FOX_HEAD_DIM
Q_BLOCK = 128
SSD_HEAD_DIM = 64
SSD_HEADS = MIX_WIDTH // SSD_HEAD_DIM
SSD_GROUPS = 4
SSD_STATE = 128
SSD_CONV = 4
SSD_CONV_DIM = MIX_WIDTH + 2 * SSD_GROUPS * SSD_STATE
SSD_CHUNK = 128
HGRN_HEAD_DIM = 128
HGRN_HEADS = MIX_WIDTH // HGRN_HEAD_DIM
HGRN_CHUNK = 64
N_BRANCH = 3
GATE_RANK = 512
D_FF = 4 * D_MODEL
SPLIT_WIDTHS = (MIX_WIDTH, MIX_WIDTH, MIX_WIDTH, FOX_HEADS,
                MIX_WIDTH, SSD_CONV_DIM, SSD_HEADS,
                MIX_WIDTH, MIX_WIDTH, MIX_WIDTH, MIX_WIDTH,
                GATE_RANK)
IN_PROJ_DIM = sum(SPLIT_WIDTHS)
DEEPNORM_ALPHA = (2 * DEPTH) ** 0.25
DEEPNORM_BETA = (8 * DEPTH) ** -0.25
LN_EPS = 1e-5
RMS_EPS = 1e-6

kernel_name = "fox_ssd_hgrn2_gated_hybrid_deepnorm"


def _split_points():
    pts, acc = [], 0
    for w in SPLIT_WIDTHS[:-1]:
        acc += w
        pts.append(acc)
    return pts


def layer_norm(x, g, b):
    xf = x.astype(jnp.float32)
    mu = jnp.mean(xf, axis=-1, keepdims=True)
    var = jnp.mean(jnp.square(xf - mu), axis=-1, keepdims=True)
    y = (xf - mu) * lax.rsqrt(var + LN_EPS) * g.astype(jnp.float32) + b.astype(jnp.float32)
    return y.astype(x.dtype)


def rms_norm(x, g):
    xf = x.astype(jnp.float32)
    return xf * lax.rsqrt(jnp.mean(xf * xf, axis=-1, keepdims=True) + RMS_EPS) * g.astype(jnp.float32)


def causal_depthwise_conv(x, w, b):
    k, c = w.shape
    y = lax.conv_general_dilated(x, w[:, None, :], window_strides=(1,), padding=[(k - 1, 0)],
                                 dimension_numbers=('NWC', 'WIO', 'NWC'), feature_group_count=c)
    return y + b


def fox_attention(q, k, v, log_f):
    bsz, s_len, h, dh = q.shape
    nb = s_len // Q_BLOCK
    c = jnp.cumsum(log_f, axis=1).transpose(0, 2, 1)
    q_blocks = jnp.moveaxis(q.reshape(bsz, nb, Q_BLOCK, h, dh), 1, 0)
    c_blocks = jnp.moveaxis(c.reshape(bsz, h, nb, Q_BLOCK), 2, 0)
    key_pos = jnp.arange(s_len)
    scale = dh ** -0.5

    def one_block(args):
        idx, q_i, c_i = args
        s = jnp.einsum('bqhd,bkhd->bhqk', q_i, k).astype(jnp.float32) * scale
        s = s + (c_i[..., :, None] - c[:, :, None, :])
        q_pos = idx * Q_BLOCK + jnp.arange(Q_BLOCK)
        s = jnp.where(key_pos[None, :] <= q_pos[:, None], s, -jnp.inf)
        p = jax.nn.softmax(s, axis=-1)
        return jnp.einsum('bhqk,bkhd->bqhd', p.astype(v.dtype), v)

    o = lax.map(one_block, (jnp.arange(nb), q_blocks, c_blocks))
    return jnp.moveaxis(o, 0, 1).reshape(bsz, s_len, h, dh)


def ssd_chunked(xh, dt, a_neg, bm, cm):
    bsz, s_len, h, p = xh.shape
    g, n = bm.shape[2], bm.shape[3]
    r = h // g
    L = SSD_CHUNK
    nc = s_len // L
    x_c = (xh * dt[..., None]).reshape(bsz, nc, L, g, r, p)
    a_c = (dt * a_neg).reshape(bsz, nc, L, g, r)
    a_cum = jnp.cumsum(a_c, axis=2).transpose(0, 1, 3, 4, 2)
    b_c = bm.reshape(bsz, nc, L, g, n)
    c_c = cm.reshape(bsz, nc, L, g, n)
    causal = jnp.tril(jnp.ones((L, L), dtype=bool))
    seg = a_cum[..., :, None] - a_cum[..., None, :]
    decay = jnp.exp(jnp.where(causal, seg, -jnp.inf))
    cb = jnp.einsum('bclgn,bcsgn->bcgls', c_c, b_c)
    y_diag = jnp.einsum('bcgls,bcgrls,bcsgrp->bclgrp', cb, decay, x_c)
    decay_end = jnp.exp(a_cum[..., -1:] - a_cum)
    states = jnp.einsum('bcsgn,bcgrs,bcsgrp->bcgrpn', b_c, decay_end, x_c)
    chunk_decay = jnp.exp(a_cum[..., -1])

    def carry_step(hs, inp):
        st, dc = inp
        return dc[..., None, None] * hs + st, hs

    h0 = jnp.zeros(states.shape[:1] + states.shape[2:], states.dtype)
    _, h_in = lax.scan(carry_step, h0, (jnp.moveaxis(states, 1, 0), jnp.moveaxis(chunk_decay, 1, 0)))
    h_in = jnp.moveaxis(h_in, 0, 1)
    y_off = jnp.einsum('bclgn,bcgrpn,bcgrl->bclgrp', c_c, h_in, jnp.exp(a_cum))
    return (y_diag + y_off).reshape(bsz, s_len, h, p)


def hgrn2_chunked(q, k, v, log_f):
    bsz, s_len, h, kd = q.shape
    vd = v.shape[-1]
    L = HGRN_CHUNK
    nc = s_len // L

    def to_chunks(t):
        return jnp.moveaxis(t.reshape(bsz, nc, L, *t.shape[2:]), 1, 0)

    causal = jnp.tril(jnp.ones((L, L), dtype=bool))[None, :, :, None, None]

    def chunk_step(state, inp):
        q_c, k_c, v_c, g_c = inp
        b_cum = jnp.cumsum(g_c, axis=1)
        o_inter = jnp.einsum('blhk,bhkv->blhv', q_c * jnp.exp(b_cum), state)
        seg = jnp.where(causal, b_cum[:, :, None] - b_cum[:, None, :], -jnp.inf)
        attn = jnp.einsum('bthk,bshk,btshk->bhts', q_c, k_c, jnp.exp(seg))
        o_intra = jnp.einsum('bhts,bshv->bthv', attn, v_c)
        b_last = b_cum[:, -1]
        state = jnp.exp(b_last)[..., None] * state + jnp.einsum(
            'bshk,bshv->bhkv', k_c * jnp.exp(b_last[:, None] - b_cum), v_c)
        return state, o_inter + o_intra

    s0 = jnp.zeros((bsz, h, kd, vd), jnp.float32)
    _, o = lax.scan(chunk_step, s0, (to_chunks(q), to_chunks(k), to_chunks(v), to_chunks(log_f)))
    return jnp.moveaxis(o, 0, 1).reshape(bsz, s_len, h, vd)


def setup_inputs(seed: int = 0) -> dict:
    key = jax.random.key(seed)
    ks = jax.random.split(key, 24)
    f32 = jnp.float32
    nrm = lambda k, shape, s: jax.random.normal(k, shape, f32) * s
    dt = jnp.exp(jax.random.uniform(ks[5], (DEPTH, SSD_HEADS), f32) * (np.log(0.1) - np.log(1e-3)) + np.log(1e-3))
    return {
        "x": nrm(ks[0], (BATCH, SEQ, D_MODEL), 1.0),
        "w_in": nrm(ks[1], (DEPTH, D_MODEL, IN_PROJ_DIM), D_MODEL ** -0.5),
        "fox_f_bias": 1.0 + nrm(ks[2], (DEPTH, FOX_HEADS), 0.1),
        "ssd_conv_w": nrm(ks[3], (DEPTH, SSD_CONV, SSD_CONV_DIM), SSD_CONV ** -0.5),
        "ssd_conv_b": nrm(ks[4], (DEPTH, SSD_CONV_DIM), 0.01),
        "ssd_dt_bias": dt + jnp.log(-jnp.expm1(-dt)),
        "ssd_a_log": jnp.log(jax.random.uniform(ks[6], (DEPTH, SSD_HEADS), f32, 1.0, 16.0)),
        "ssd_d": 1.0 + nrm(ks[7], (DEPTH, SSD_HEADS), 0.01),
        "ssd_norm_g": 1.0 + nrm(ks[8], (DEPTH, MIX_WIDTH), 0.01),
        "hgrn_lower_bound": nrm(ks[9], (DEPTH, MIX_WIDTH), 0.1),
        "hgrn_norm_g": 1.0 + nrm(ks[10], (DEPTH, HGRN_HEAD_DIM), 0.01),
        "w_gate": nrm(ks[11], (DEPTH, N_BRANCH, GATE_RANK, D_MODEL), GATE_RANK ** -0.5),
        "b_gate": nrm(ks[12], (DEPTH, N_BRANCH, D_MODEL), 0.01),
        "w_branch": nrm(ks[13], (DEPTH, N_BRANCH, MIX_WIDTH, D_MODEL), MIX_WIDTH ** -0.5),
        "w_out": nrm(ks[14], (DEPTH, D_MODEL, D_MODEL), DEEPNORM_BETA * D_MODEL ** -0.5),
        "ln1_g": 1.0 + nrm(ks[15], (DEPTH, D_MODEL), 0.01),
        "ln1_b": nrm(ks[16], (DEPTH, D_MODEL), 0.01),
        "w_up": nrm(ks[17], (DEPTH, D_MODEL, D_FF), D_MODEL ** -0.5),
        "w_down": nrm(ks[18], (DEPTH, D_FF, D_MODEL), DEEPNORM_BETA * D_FF ** -0.5),
        "ln2_g": 1.0 + nrm(ks[19], (DEPTH, D_MODEL), 0.01),
        "ln2_b": nrm(ks[20], (DEPTH, D_MODEL), 0.01),
    }


def reference(x, w_in, fox_f_bias, ssd_conv_w, ssd_conv_b, ssd_dt_bias, ssd_a_log, ssd_d,
              ssd_norm_g, hgrn_lower_bound, hgrn_norm_g, w_gate, b_gate, w_branch, w_out,
              ln1_g, ln1_b, w_up, w_down, ln2_g, ln2_b):
    bsz, s_len, _ = x.shape
    f32 = jnp.float32
    points = _split_points()
    lb_all = jnp.cumsum(jax.nn.softmax(hgrn_lower_bound.astype(f32), axis=0), axis=0)
    lb_all = lb_all - lb_all[0]

    for l in range(DEPTH):
        u = x @ w_in[l]
        (a_q, a_k, a_v, a_f, s_z, s_xbc, s_dt,
         h_q, h_f, h_i, h_g, gate_lat) = jnp.split(u, points, axis=-1)

        shp = (bsz, s_len, FOX_HEADS, FOX_HEAD_DIM)
        fox_log_f = jax.nn.log_sigmoid((a_f + fox_f_bias[l]).astype(f32))
        y_a = fox_attention(a_q.reshape(shp), a_k.reshape(shp), a_v.reshape(shp), fox_log_f)
        y_a = y_a.reshape(bsz, s_len, MIX_WIDTH).astype(x.dtype)

        xbc = jax.nn.silu(causal_depthwise_conv(s_xbc, ssd_conv_w[l], ssd_conv_b[l]))
        s_x, s_b, s_c = jnp.split(xbc, [MIX_WIDTH, MIX_WIDTH + SSD_GROUPS * SSD_STATE], axis=-1)
        xh = s_x.reshape(bsz, s_len, SSD_HEADS, SSD_HEAD_DIM)
        dt = jax.nn.softplus((s_dt + ssd_dt_bias[l]).astype(f32))
        a_neg = -jnp.exp(ssd_a_log[l].astype(f32))
        y_b = ssd_chunked(xh, dt, a_neg,
                          s_b.reshape(bsz, s_len, SSD_GROUPS, SSD_STATE),
                          s_c.reshape(bsz, s_len, SSD_GROUPS, SSD_STATE))
        y_b = y_b + ssd_d[l][:, None] * xh
        yz = (y_b.reshape(bsz, s_len, MIX_WIDTH) * jax.nn.silu(s_z)).reshape(
            bsz, s_len, SSD_GROUPS, MIX_WIDTH // SSD_GROUPS)
        y_b = rms_norm(yz, ssd_norm_g[l].reshape(SSD_GROUPS, MIX_WIDTH // SSD_GROUPS))
        y_b = y_b.reshape(bsz, s_len, MIX_WIDTH).astype(x.dtype)

        hshp = (bsz, s_len, HGRN_HEADS, HGRN_HEAD_DIM)
        lb = lb_all[l].reshape(HGRN_HEADS, HGRN_HEAD_DIM)
        f = lb + (1.0 - lb) * jax.nn.sigmoid(h_f.reshape(hshp).astype(f32))
        o_c = hgrn2_chunked(jax.nn.silu(h_q.reshape(hshp)), 1.0 - f, h_i.reshape(hshp), jnp.log(f))
        y_c = rms_norm(o_c, hgrn_norm_g[l]) * jax.nn.silu(h_g.reshape(hshp).astype(f32))
        y_c = y_c.reshape(bsz, s_len, MIX_WIDTH).astype(x.dtype)

        merged = None
        for i, y in enumerate((y_a, y_b, y_c)):
            gate = jax.nn.sigmoid(gate_lat @ w_gate[l, i] + b_gate[l, i])
            contrib = gate * (y @ w_branch[l, i])
            merged = contrib if merged is None else merged + contrib
        mix = merged @ w_out[l]
        x = layer_norm(DEEPNORM_ALPHA * x + mix, ln1_g[l], ln1_b[l])

        hmid = jnp.square(jax.nn.relu(x @ w_up[l]))
        x = layer_norm(DEEPNORM_ALPHA * x + hmid @ w_down[l], ln2_g[l], ln2_b[l])
    return x
```

```python
import functools

import numpy as np
import jax
import jax.numpy as jnp
from jax import lax
from jax.experimental import pallas as pl
from jax.experimental.pallas import tpu as pltpu

F32 = jnp.float32
BF16 = jnp.bfloat16

LANE = 128
SUBLANE = 8
MXU_DIM = 256
VMEM_BYTES = 64 * 1024 * 1024
VMEM_CAP = VMEM_BYTES - 6 * 1024 * 1024

FOX_HEAD_DIM = 128
SSD_HEAD_DIM = 64
SSD_STATE = 128
SSD_CHUNK = 128
SSD_GROUP_WIDTH = 256
HGRN_HEAD_DIM = 128
HGRN_BLOCK = 256
LN_EPS = 1e-5
RMS_EPS = 1e-6
DT_COL = 8


def _pick_tile(n, cap, quantum=MXU_DIM):
    for q in (quantum, LANE):
        best = 0
        t = q
        while t <= min(n, cap):
            if n % t == 0:
                best = t
            t += q
        if best:
            return best
    return n


def _params(semantics, vmem_estimate):
    limit = int(min(max(vmem_estimate * 5 // 4 + (2 << 20), 16 << 20), VMEM_CAP))
    return pltpu.CompilerParams(dimension_semantics=semantics, vmem_limit_bytes=limit)


def _sigmoid(x):
    return 1.0 / (1.0 + jnp.exp(-x))


def _split3(x):
    hi = x.astype(BF16)
    r = x - hi.astype(F32)
    mid = r.astype(BF16)
    lo = (r - mid.astype(F32)).astype(BF16)
    return hi, mid, lo


def _dot_sel_lhs(sel, x):
    hi, mid, lo = _split3(x)
    d = lambda p: jnp.dot(sel, p, preferred_element_type=F32)
    return d(hi) + (d(mid) + d(lo))


def _dot_sel_rhs(x, sel):
    hi, mid, lo = _split3(x)
    d = lambda p: jnp.dot(p, sel, preferred_element_type=F32)
    return d(hi) + (d(mid) + d(lo))


def _dot_nt(a, b):
    return lax.dot_general(a, b, (((1,), (1,)), ((), ())), preferred_element_type=F32)


def _dot_tn(a, b):
    return lax.dot_general(a, b, (((0,), (0,)), ((), ())), preferred_element_type=F32)


def _mm_kernel(a_ref, w_ref, o_ref, *, act):
    acc = jnp.dot(a_ref[...], w_ref[...], preferred_element_type=F32)
    if act == "relu2":
        acc = jnp.square(jnp.maximum(acc, 0.0))
    o_ref[...] = acc.astype(o_ref.dtype)


def _matmul(a, w, out_dtype, act=None, tm_cap=1024, tn_cap=1024):
    m, k = a.shape
    n = w.shape[1]
    tm = _pick_tile(m, tm_cap)
    tn = _pick_tile(n, tn_cap)
    osz = jnp.dtype(out_dtype).itemsize
    est = 2 * (tm * k * 2 + k * tn * 2 + tm * tn * osz) + tm * tn * 4
    return pl.pallas_call(
        functools.partial(_mm_kernel, act=act),
        grid=(m // tm, n // tn),
        in_specs=[pl.BlockSpec((tm, k), lambda i, j: (i, 0)),
                  pl.BlockSpec((k, tn), lambda i, j: (0, j))],
        out_specs=pl.BlockSpec((tm, tn), lambda i, j: (i, j)),
        out_shape=jax.ShapeDtypeStruct((m, n), out_dtype),
        compiler_params=_params(("parallel", "parallel"), est),
        name="matmul_" + (act or "plain"),
    )(a, w)


def _mm_acc_kernel(a_ref, w_ref, o_ref, acc_ref):
    kk = pl.program_id(2)

    @pl.when(kk == 0)
    def _():
        acc_ref[...] = jnp.zeros_like(acc_ref)

    acc_ref[...] += jnp.dot(a_ref[...], w_ref[...], preferred_element_type=F32)

    @pl.when(kk == pl.num_programs(2) - 1)
    def _():
        o_ref[...] = acc_ref[...].astype(o_ref.dtype)


def _matmul_ktiled(a, w, out_dtype, tm_cap=1024, tn_cap=2048, tk_cap=1024):
    m, k = a.shape
    n = w.shape[1]
    tm = _pick_tile(m, tm_cap)
    tn = _pick_tile(n, tn_cap)
    tk = _pick_tile(k, tk_cap)
    osz = jnp.dtype(out_dtype).itemsize
    est = 2 * (tm * tk * 2 + tk * tn * 2 + tm * tn * osz) + 2 * tm * tn * 4
    return pl.pallas_call(
        _mm_acc_kernel,
        grid=(m // tm, n // tn, k // tk),
        in_specs=[pl.BlockSpec((tm, tk), lambda i, j, kk: (i, kk)),
                  pl.BlockSpec((tk, tn), lambda i, j, kk: (kk, j))],
        out_specs=pl.BlockSpec((tm, tn), lambda i, j, kk: (i, j)),
        out_shape=jax.ShapeDtypeStruct((m, n), out_dtype),
        scratch_shapes=[pltpu.VMEM((tm, tn), F32)],
        compiler_params=_params(("parallel", "parallel", "arbitrary"), est),
        name="matmul_ktiled",
    )(a, w)


def _ln_kernel(x_ref, y_ref, g_ref, b_ref, o_ref, ob_ref, *, alpha, eps):
    z = alpha * x_ref[...] + y_ref[...]
    mu = jnp.mean(z, axis=-1, keepdims=True)
    zc = z - mu
    var = jnp.mean(zc * zc, axis=-1, keepdims=True)
    out = zc * lax.rsqrt(var + eps) * g_ref[...] + b_ref[...]
    o_ref[...] = out
    ob_ref[...] = out.astype(BF16)


def _residual_layernorm(x, y, g, b, alpha):
    m, d = x.shape
    tm = _pick_tile(m, 256, SUBLANE)
    est = 2 * tm * d * (4 + 4 + 4 + 2) + 4 * tm * d * 4
    row = pl.BlockSpec((tm, d), lambda i: (i, 0))
    vec = pl.BlockSpec((1, d), lambda i: (0, 0))
    return pl.pallas_call(
        functools.partial(_ln_kernel, alpha=alpha, eps=LN_EPS),
        grid=(m // tm,),
        in_specs=[row, row, vec, vec],
        out_specs=[row, row],
        out_shape=[jax.ShapeDtypeStruct((m, d), F32), jax.ShapeDtypeStruct((m, d), BF16)],
        compiler_params=_params(("parallel",), est),
        name="residual_layernorm",
    )(x, y, g.reshape(1, d), b.reshape(1, d))


def _gates_kernel(x_ref, w_ref, bias_ref, tri_ref, aux_ref, ct_ref, carry_ref, *, n_fox):
    blk = pl.program_id(1)

    @pl.when(blk == 0)
    def _():
        carry_ref[...] = jnp.zeros_like(carry_ref)

    s = jnp.dot(x_ref[...], w_ref[...], preferred_element_type=F32) + bias_ref[...]
    soft = jnp.log(1.0 + jnp.exp(-jnp.abs(s)))
    log_sig = jnp.minimum(s, 0.0) - soft
    softplus = jnp.maximum(s, 0.0) + soft
    csum = carry_ref[...] + _dot_sel_lhs(tri_ref[...], log_sig)
    carry_ref[...] = csum[-1:, :]
    lane = lax.broadcasted_iota(jnp.int32, s.shape, 1)
    aux = jnp.where(lane < n_fox, csum, softplus)
    aux_ref[...] = aux
    ct_ref[0] = aux.T[:SUBLANE, :]


def _small_gates(x_bf, w_small, bias_small, bsz, s_len, n_fox):
    m, d = x_bf.shape
    t = _pick_tile(s_len, 512, LANE)
    nblk = s_len // t
    tri = jnp.asarray(np.tril(np.ones((t, t), np.float32)), BF16)
    est = 2 * (t * d * 2 + d * LANE * 2 + t * t * 2 + t * LANE * 4 * 2) + 8 * t * LANE * 4
    return pl.pallas_call(
        functools.partial(_gates_kernel, n_fox=n_fox),
        grid=(bsz, nblk),
        in_specs=[pl.BlockSpec((t, d), lambda b, i: (b * nblk + i, 0)),
                  pl.BlockSpec((d, LANE), lambda b, i: (0, 0)),
                  pl.BlockSpec((1, LANE), lambda b, i: (0, 0)),
                  pl.BlockSpec((t, t), lambda b, i: (0, 0))],
        out_specs=[pl.BlockSpec((t, LANE), lambda b, i: (b * nblk + i, 0)),
                   pl.BlockSpec((1, SUBLANE, t), lambda b, i: (b, 0, i))],
        out_shape=[jax.ShapeDtypeStruct((m, LANE), F32),
                   jax.ShapeDtypeStruct((bsz, SUBLANE, s_len), F32)],
        scratch_shapes=[pltpu.VMEM((1, LANE), F32)],
        compiler_params=_params(("parallel", "arbitrary"), est),
        name="small_gates",
    )(x_bf, w_small, bias_small, tri)


def _fox_kernel(q_ref, k_ref, v_ref, aux_ref, ct_ref, o_ref, *, tq, scale):
    h = pl.program_id(1)
    i = pl.program_id(2)
    q = q_ref[...].astype(BF16)
    lane = lax.broadcasted_iota(jnp.int32, (tq, LANE), 1)
    c_t = jnp.sum(jnp.where(lane == h, aux_ref[...], 0.0), axis=1, keepdims=True)

    def block(j, carry, masked):
        m_run, l_run, acc = carry
        ks = pl.multiple_of(j * tq, tq)
        kb = k_ref[pl.ds(ks, tq), :].astype(BF16)
        vb = v_ref[pl.ds(ks, tq), :].astype(BF16)
        c_s = ct_ref[0, pl.ds(h, 1), pl.ds(ks, tq)]
        s = _dot_nt(q, kb) * scale + (c_t - c_s)
        if masked:
            row = lax.broadcasted_iota(jnp.int32, (tq, tq), 0)
            col = lax.broadcasted_iota(jnp.int32, (tq, tq), 1)
            s = jnp.where(col <= row, s, -jnp.inf)
        m_new = jnp.maximum(m_run, jnp.max(s, axis=1, keepdims=True))
        corr = jnp.exp(m_run - m_new)
        p = jnp.exp(s - m_new)
        l_new = corr * l_run + jnp.sum(p, axis=1, keepdims=True)
        acc = corr * acc + jnp.dot(p.astype(BF16), vb, preferred_element_type=F32)
        return m_new, l_new, acc

    init = (jnp.full((tq, 1), -jnp.inf, F32), jnp.zeros((tq, 1), F32), jnp.zeros((tq, FOX_HEAD_DIM), F32))
    carry = block(i, init, True)
    _, l_fin, acc = lax.fori_loop(0, i, lambda j, c: block(j, c, False), carry)
    o_ref[...] = (acc / l_fin).astype(o_ref.dtype)


def _fox_attention(u, aux, ct, bsz, s_len, mix, n_heads):
    m = u.shape[0]
    tq = _pick_tile(s_len, 512, LANE)
    nq = s_len // tq
    hb = mix // FOX_HEAD_DIM
    est = 2 * (tq * LANE * 4 * 2 + 2 * s_len * LANE * 4 + SUBLANE * s_len * 4 + tq * LANE * 2) + 10 * tq * tq * 4
    return pl.pallas_call(
        functools.partial(_fox_kernel, tq=tq, scale=FOX_HEAD_DIM ** -0.5),
        grid=(bsz, n_heads, nq),
        in_specs=[pl.BlockSpec((tq, FOX_HEAD_DIM), lambda b, h, i: (b * nq + i, h)),
                  pl.BlockSpec((s_len, FOX_HEAD_DIM), lambda b, h, i: (b, hb + h)),
                  pl.BlockSpec((s_len, FOX_HEAD_DIM), lambda b, h, i: (b, 2 * hb + h)),
                  pl.BlockSpec((tq, LANE), lambda b, h, i: (b * nq + i, 0)),
                  pl.BlockSpec((1, SUBLANE, s_len), lambda b, h, i: (b, 0, 0))],
        out_specs=pl.BlockSpec((tq, FOX_HEAD_DIM), lambda b, h, i: (b * nq + i, h)),
        out_shape=jax.ShapeDtypeStruct((m, mix), BF16),
        compiler_params=_params(("parallel", "parallel", "parallel"), est),
        name="fox_attention",
    )(u, u, u, aux, ct)


def _ssd_kernel(z_ref, xbc_ref, aux_ref, cw_ref, cb_ref, aneg_ref, e_ref, tri_ref, d_ref, g_ref,
                o_ref, xext_ref, state_ref, *, mix, groups, kconv):
    L = SSD_CHUNK
    N = SSD_STATE
    GW = SSD_GROUP_WIDTH
    c = pl.program_id(1)

    @pl.when(c == 0)
    def _():
        state_ref[...] = jnp.zeros_like(state_ref)
        xext_ref[0:SUBLANE, :] = jnp.zeros((SUBLANE, xext_ref.shape[1]), F32)

    xext_ref[SUBLANE:SUBLANE + L, :] = xbc_ref[...]
    conv = cb_ref[...] + cw_ref[0:1, :] * xext_ref[pl.ds(SUBLANE - (kconv - 1), L), :]
    for j in range(1, kconv):
        conv = conv + cw_ref[j:j + 1, :] * xext_ref[pl.ds(SUBLANE - (kconv - 1) + j, L), :]
    xext_ref[0:SUBLANE, :] = xext_ref[L:L + SUBLANE, :]
    xbc = conv * _sigmoid(conv)
    xs = xbc[:, :mix]

    aux = aux_ref[...]
    dta = aux * aneg_ref[...]
    a_all = _dot_sel_lhs(tri_ref[...], dta)
    a_all_t = a_all.T
    e = e_ref[...]
    dt_exp = _dot_sel_rhs(aux, e)
    a_exp = _dot_sel_rhs(a_all, e)
    a_last = a_exp[L - 1:L, :]
    ea = jnp.exp(a_exp)
    xc = xs * dt_exp
    xd = (xc * jnp.exp(a_last - a_exp)).astype(BF16)

    row = lax.broadcasted_iota(jnp.int32, (L, L), 0)
    col = lax.broadcasted_iota(jnp.int32, (L, L), 1)
    causal = col <= row
    lane_lo = lax.broadcasted_iota(jnp.int32, (L, LANE), 1) < SSD_HEAD_DIM

    y_parts = []
    for g in range(groups):
        bg = xbc[:, mix + g * N: mix + (g + 1) * N].astype(BF16)
        cg = xbc[:, mix + (groups + g) * N: mix + (groups + g + 1) * N].astype(BF16)
        cbm = _dot_nt(cg, bg)
        st = state_ref[g]
        gsl = slice(g * GW, (g + 1) * GW)
        y_off = jnp.dot(cg, st.astype(BF16), preferred_element_type=F32) * ea[:, gsl]
        state_ref[g] = ea[L - 1:L, gsl] * st + _dot_tn(bg, xd[:, gsl])
        y_cols = []
        for pair in range(GW // LANE):
            c0 = g * GW + pair * LANE
            xcp = xc[:, c0:c0 + LANE]
            acc = None
            for half in range(LANE // SSD_HEAD_DIM):
                hh = DT_COL + c0 // SSD_HEAD_DIM + half
                seg = a_all[:, hh:hh + 1] - a_all_t[hh:hh + 1, :]
                dec = jnp.where(causal, jnp.exp(jnp.minimum(seg, 0.0)), 0.0)
                gm = (cbm * dec).astype(BF16)
                xm = jnp.where(lane_lo if half == 0 else jnp.logical_not(lane_lo), xcp, 0.0).astype(BF16)
                part = jnp.dot(gm, xm, preferred_element_type=F32)
                acc = part if acc is None else acc + part
            y_cols.append(acc)
        y_diag = jnp.concatenate(y_cols, axis=1)
        y = y_diag + y_off + d_ref[:, gsl] * xs[:, gsl]
        zg = z_ref[:, gsl]
        yz = y * (zg * _sigmoid(zg))
        ms = jnp.mean(yz * yz, axis=1, keepdims=True)
        y_parts.append(yz * lax.rsqrt(ms + RMS_EPS) * g_ref[:, gsl])
    o_ref[...] = jnp.concatenate(y_parts, axis=1).astype(o_ref.dtype)


def _ssd_mixer(u, aux, conv_w, conv_b, a_neg_row, d_exp, norm_g, bsz, s_len, mix, n_heads):
    m = u.shape[0]
    L = SSD_CHUNK
    groups = mix // SSD_GROUP_WIDTH
    width = mix + 2 * groups * SSD_STATE
    kconv = conv_w.shape[0]
    nchunk = s_len // L
    z_blk = (3 * mix) // mix
    xbc_blk = (4 * mix) // width
    expand = np.zeros((LANE, mix), np.float32)
    for hh in range(n_heads):
        expand[DT_COL + hh, hh * SSD_HEAD_DIM:(hh + 1) * SSD_HEAD_DIM] = 1.0
    tri = np.tril(np.ones((L, L), np.float32))
    est = 2 * (L * mix * 4 + L * width * 4 + L * LANE * 4 + LANE * mix * 2 + L * mix * 2) + 24 * L * width * 4
    return pl.pallas_call(
        functools.partial(_ssd_kernel, mix=mix, groups=groups, kconv=kconv),
        grid=(bsz, nchunk),
        in_specs=[pl.BlockSpec((L, mix), lambda b, c: (b * nchunk + c, z_blk)),
                  pl.BlockSpec((L, width), lambda b, c: (b * nchunk + c, xbc_blk)),
                  pl.BlockSpec((L, LANE), lambda b, c: (b * nchunk + c, 0)),
                  pl.BlockSpec((kconv, width), lambda b, c: (0, 0)),
                  pl.BlockSpec((1, width), lambda b, c: (0, 0)),
                  pl.BlockSpec((1, LANE), lambda b, c: (0, 0)),
                  pl.BlockSpec((LANE, mix), lambda b, c: (0, 0)),
                  pl.BlockSpec((L, L), lambda b, c: (0, 0)),
                  pl.BlockSpec((1, mix), lambda b, c: (0, 0)),
                  pl.BlockSpec((1, mix), lambda b, c: (0, 0))],
        out_specs=pl.BlockSpec((L, mix), lambda b, c: (b * nchunk + c, 0)),
        out_shape=jax.ShapeDtypeStruct((m, mix), BF16),
        scratch_shapes=[pltpu.VMEM((L + SUBLANE, width), F32),
                        pltpu.VMEM((groups, SSD_STATE, SSD_GROUP_WIDTH), F32)],
        compiler_params=_params(("parallel", "arbitrary"), est),
        name="ssd_mixer",
    )(u, u, aux, conv_w, conv_b.reshape(1, width), a_neg_row, jnp.asarray(expand, BF16),
      jnp.asarray(tri, BF16), d_exp.reshape(1, mix), norm_g.reshape(1, mix))


def _hgrn_tables(blk):
    nlev = int(np.log2(blk))
    tri = np.tril(np.ones((blk, blk), np.float32))
    rows = [tri]
    t = np.arange(blk)
    for l in range(1, nlev + 1):
        size = 1 << l
        mid = (t // size) * size + size // 2 - 1
        rows.append(tri - tri[mid])
    wall = np.concatenate(rows, axis=0)
    x = t[:, None] ^ t[None, :]
    lv = np.where(x > 0, np.floor(np.log2(np.maximum(x, 1))).astype(np.int32) + 1, 0)
    lv = np.where(t[None, :] > t[:, None], -1, lv).astype(np.int32)
    return wall, lv, nlev


def _hgrn_kernel(q_ref, f_ref, i_ref, g_ref, lb_ref, ng_ref, wall_ref, lv_ref, o_ref, state_ref,
                 *, blk, nlev, n_heads):
    c = pl.program_id(1)

    @pl.when(c == 0)
    def _():
        state_ref[...] = jnp.zeros_like(state_ref)

    def head(h, carry):
        sl = pl.ds(pl.multiple_of(h * HGRN_HEAD_DIM, HGRN_HEAD_DIM), HGRN_HEAD_DIM)
        hq = q_ref[:, sl]
        lb = lb_ref[:, sl]
        f = lb + (1.0 - lb) * _sigmoid(f_ref[:, sl])
        logf = jnp.log(f)
        k = 1.0 - f
        q = hq * _sigmoid(hq)
        vb = i_ref[:, sl].astype(BF16)
        dall = _dot_sel_lhs(wall_ref[...], logf)
        b = dall[0:blk]
        lv = lv_ref[...]
        attn = jnp.where(lv == 0, _dot_nt(q.astype(BF16), k.astype(BF16)), 0.0)
        for l in range(1, nlev + 1):
            fac = jnp.exp(-jnp.abs(dall[l * blk:(l + 1) * blk]))
            al = _dot_nt((q * fac).astype(BF16), (k * fac).astype(BF16))
            attn = jnp.where(lv == l, al, attn)
        st = state_ref[h]
        o = jnp.dot(attn.astype(BF16), vb, preferred_element_type=F32)
        o = o + _dot_nt((q * jnp.exp(b)).astype(BF16), st.astype(BF16))
        b_last = b[blk - 1:blk, :]
        kd = (k * jnp.exp(b_last - b)).astype(BF16)
        state_ref[h] = jnp.exp(b_last) * st + _dot_tn(vb, kd)
        ms = jnp.mean(o * o, axis=1, keepdims=True)
        hg = g_ref[:, sl]
        y = o * lax.rsqrt(ms + RMS_EPS) * ng_ref[:, sl] * (hg * _sigmoid(hg))
        o_ref[:, sl] = y.astype(o_ref.dtype)
        return carry

    lax.fori_loop(0, n_heads, head, 0)


def _hgrn_mixer(u, lb, norm_g, bsz, s_len, mix, col0):
    m = u.shape[0]
    blk = _pick_tile(s_len, HGRN_BLOCK, LANE)
    n_heads = mix // HGRN_HEAD_DIM
    wall, lv, nlev = _hgrn_tables(blk)
    nblk = s_len // blk
    cb = col0 // mix
    spec = lambda k: pl.BlockSpec((blk, mix), lambda b, c, k=k: (b * nblk + c, cb + k))
    vec = pl.BlockSpec((1, mix), lambda b, c: (0, 0))
    est = (2 * (4 * blk * mix * 4 + blk * mix * 2 + wall.size * 2 + blk * blk * 4)
           + 6 * (nlev + 1) * blk * LANE * 4 + 12 * blk * blk * 4)
    return pl.pallas_call(
        functools.partial(_hgrn_kernel, blk=blk, nlev=nlev, n_heads=n_heads),
        grid=(bsz, nblk),
        in_specs=[spec(0), spec(1), spec(2), spec(3), vec, vec,
                  pl.BlockSpec(wall.shape, lambda b, c: (0, 0)),
                  pl.BlockSpec((blk, blk), lambda b, c: (0, 0))],
        out_specs=pl.BlockSpec((blk, mix), lambda b, c: (b * nblk + c, 0)),
        out_shape=jax.ShapeDtypeStruct((m, mix), BF16),
        scratch_shapes=[pltpu.VMEM((n_heads, HGRN_HEAD_DIM, HGRN_HEAD_DIM), F32)],
        compiler_params=_params(("parallel", "arbitrary"), est),
        name="hgrn2_mixer",
    )(u, u, u, u, lb.reshape(1, mix), norm_g.reshape(1, mix), jnp.asarray(wall, BF16), jnp.asarray(lv))


def _merge_kernel(gl_ref, ya_ref, yb_ref, yc_ref, wg_ref, bg_ref, wb_ref, o_ref):
    gl = gl_ref[...].astype(BF16)
    acc = None
    for i, y_ref in enumerate((ya_ref, yb_ref, yc_ref)):
        gate = _sigmoid(jnp.dot(gl, wg_ref[i], preferred_element_type=F32) + bg_ref[i])
        contrib = gate * jnp.dot(y_ref[...], wb_ref[i], preferred_element_type=F32)
        acc = contrib if acc is None else acc + contrib
    o_ref[...] = acc.astype(o_ref.dtype)


def _gated_merge(u, gate_col, rank, ya, yb, yc, w_gate, b_gate, w_branch):
    m = u.shape[0]
    mix = ya.shape[1]
    nbr, _, d = w_gate.shape
    tm = _pick_tile(m, 1024)
    tn = _pick_tile(d, 512)
    ybs = pl.BlockSpec((tm, mix), lambda i, j: (i, 0))
    est = 2 * (tm * rank * 4 + 3 * tm * mix * 2 + nbr * rank * tn * 2 + nbr * mix * tn * 2 + tm * tn * 2) + 6 * tm * tn * 4
    return pl.pallas_call(
        _merge_kernel,
        grid=(m // tm, d // tn),
        in_specs=[pl.BlockSpec((tm, rank), lambda i, j: (i, gate_col // rank)),
                  ybs, ybs, ybs,
                  pl.BlockSpec((nbr, rank, tn), lambda i, j: (0, 0, j)),
                  pl.BlockSpec((nbr, 1, tn), lambda i, j: (0, 0, j)),
                  pl.BlockSpec((nbr, mix, tn), lambda i, j: (0, 0, j))],
        out_specs=pl.BlockSpec((tm, tn), lambda i, j: (i, j)),
        out_shape=jax.ShapeDtypeStruct((m, d), BF16),
        compiler_params=_params(("parallel", "parallel"), est),
        name="gated_merge",
    )(u, ya, yb, yc, w_gate, b_gate.reshape(nbr, 1, d), w_branch)


def kernel(x, w_in, fox_f_bias, ssd_conv_w, ssd_conv_b, ssd_dt_bias, ssd_a_log, ssd_d, ssd_norm_g, hgrn_lower_bound, hgrn_norm_g, w_gate, b_gate, w_branch, w_out, ln1_g, ln1_b, w_up, w_down, ln2_g, ln2_b):
    bsz, s_len, d = x.shape
    depth = w_in.shape[0]
    mix = d // 4
    n_fox = fox_f_bias.shape[1]
    n_ssd = ssd_dt_bias.shape[1]
    rank = w_gate.shape[2]
    conv_dim = ssd_conv_w.shape[2]
    alpha = (2 * depth) ** 0.25
    assert mix % SSD_GROUP_WIDTH == 0 and conv_dim == 2 * mix and n_fox <= DT_COL and DT_COL + n_ssd <= LANE
    assert n_fox * FOX_HEAD_DIM == mix and n_ssd * SSD_HEAD_DIM == mix
    assert s_len % SSD_CHUNK == 0 and (10 * mix) % rank == 0

    widths = (mix, mix, mix, n_fox, mix, conv_dim, n_ssd, mix, mix, mix, mix, rank)
    offs = np.concatenate([[0], np.cumsum(widths)])
    col = lambda k: slice(int(offs[k]), int(offs[k + 1]))
    main_groups = (0, 1, 2, 4, 5, 7, 8, 9, 10, 11)
    hgrn_col0 = 3 * mix + mix + conv_dim
    gate_col0 = hgrn_col0 + 4 * mix

    lb_all = jnp.cumsum(jax.nn.softmax(hgrn_lower_bound.astype(F32), axis=0), axis=0)
    lb_all = lb_all - lb_all[0]

    m = bsz * s_len
    xf = x.reshape(m, d).astype(F32)
    xb = xf.astype(BF16)
    pad = LANE - DT_COL - n_ssd
    for l in range(depth):
        w_main = jnp.concatenate([w_in[l][:, col(k)] for k in main_groups], axis=1).astype(BF16)
        w_small = jnp.concatenate(
            [w_in[l][:, col(3)], jnp.zeros((d, DT_COL - n_fox), F32), w_in[l][:, col(6)], jnp.zeros((d, pad), F32)],
            axis=1).astype(BF16)
        bias_small = jnp.concatenate(
            [fox_f_bias[l], jnp.zeros((DT_COL - n_fox,), F32), ssd_dt_bias[l], jnp.zeros((pad,), F32)]).reshape(1, LANE)
        a_neg_row = jnp.concatenate(
            [jnp.zeros((DT_COL,), F32), -jnp.exp(ssd_a_log[l].astype(F32)), jnp.zeros((pad,), F32)]).reshape(1, LANE)

        u = _matmul(xb, w_main, F32, tn_cap=768)
        aux, ct = _small_gates(xb, w_small, bias_small, bsz, s_len, n_fox)
        y_a = _fox_attention(u, aux, ct, bsz, s_len, mix, n_fox)
        y_b = _ssd_mixer(u, aux, ssd_conv_w[l], ssd_conv_b[l], a_neg_row,
                         jnp.repeat(ssd_d[l], SSD_HEAD_DIM), ssd_norm_g[l], bsz, s_len, mix, n_ssd)
        y_c = _hgrn_mixer(u, lb_all[l], jnp.tile(hgrn_norm_g[l], mix // HGRN_HEAD_DIM), bsz, s_len, mix, hgrn_col0)
        merged = _gated_merge(u, gate_col0, rank, y_a, y_b, y_c,
                              w_gate[l].astype(BF16), b_gate[l], w_branch[l].astype(BF16))
        mixed = _matmul(merged, w_out[l].astype(BF16), F32)
        xf, xb = _residual_layernorm(xf, mixed, ln1_g[l], ln1_b[l], alpha)
        hmid = _matmul(xb, w_up[l].astype(BF16), BF16, act="relu2")
        down = _matmul_ktiled(hmid, w_down[l].astype(BF16), F32)
        xf, xb = _residual_layernorm(xf, down, ln2_g[l], ln2_b[l], alpha)
    return xf.reshape(bsz, s_len, d).astype(x.dtype)
```

```python
import functools

import numpy as np
import jax
import jax.numpy as jnp
from jax import lax
from jax.experimental import pallas as pl
from jax.experimental.pallas import tpu as pltpu

F32 = jnp.float32
BF16 = jnp.bfloat16

LANE = 128
SUBLANE = 8
MXU_DIM = 256
VMEM_BYTES = 64 * 1024 * 1024
VMEM_CAP = VMEM_BYTES - 6 * 1024 * 1024

FOX_HEAD_DIM = 128
SSD_HEAD_DIM = 64
SSD_STATE = 128
SSD_CHUNK = 128
SSD_GROUP_WIDTH = 256
HGRN_HEAD_DIM = 128
HGRN_BLOCK = 256
HGRN_MM_LEVELS = 3
HGRN_PAIR = MXU_DIM // HGRN_HEAD_DIM
LN_EPS = 1e-5
RMS_EPS = 1e-6
DT_COL = 8


def _pick_tile(n, cap, quantum=MXU_DIM):
    for q in (quantum, LANE):
        best = 0
        t = q
        while t <= min(n, cap):
            if n % t == 0:
                best = t
            t += q
        if best:
            return best
    return n


def _params(semantics, vmem_estimate):
    limit = int(min(max(vmem_estimate * 5 // 4 + (2 << 20), 16 << 20), VMEM_CAP))
    return pltpu.CompilerParams(dimension_semantics=semantics, vmem_limit_bytes=limit)


def _sigmoid(x):
    return 1.0 / (1.0 + jnp.exp(-x))


def _split3(x):
    hi = x.astype(BF16)
    r = x - hi.astype(F32)
    mid = r.astype(BF16)
    lo = (r - mid.astype(F32)).astype(BF16)
    return hi, mid, lo


def _dot_sel_lhs(sel, x):
    hi, mid, lo = _split3(x)
    d = lambda p: jnp.dot(sel, p, preferred_element_type=F32)
    return d(hi) + (d(mid) + d(lo))


def _dot_sel_rhs(x, sel):
    hi, mid, lo = _split3(x)
    d = lambda p: jnp.dot(p, sel, preferred_element_type=F32)
    return d(hi) + (d(mid) + d(lo))


def _dot_nt(a, b):
    return lax.dot_general(a, b, (((1,), (1,)), ((), ())), preferred_element_type=F32)


def _dot_tn(a, b):
    return lax.dot_general(a, b, (((0,), (0,)), ((), ())), preferred_element_type=F32)


def _mm_kernel(a_ref, w_ref, o_ref, *, act):
    acc = jnp.dot(a_ref[...], w_ref[...], preferred_element_type=F32)
    if act == "relu2":
        acc = jnp.square(jnp.maximum(acc, 0.0))
    o_ref[...] = acc.astype(o_ref.dtype)


def _mm_res_kernel(a_ref, w_ref, r_ref, o_ref, *, alpha):
    acc = jnp.dot(a_ref[...], w_ref[...], preferred_element_type=F32)
    o_ref[...] = alpha * r_ref[...] + acc


def _matmul(a, w, out_dtype, act=None, residual=None, alpha=None, tm_cap=1024, tn_cap=1024):
    m, k = a.shape
    n = w.shape[1]
    tm = _pick_tile(m, tm_cap)
    tn = _pick_tile(n, tn_cap)
    osz = jnp.dtype(out_dtype).itemsize
    est = 2 * (tm * k * 2 + k * tn * 2 + tm * tn * osz) + tm * tn * 4
    tile = pl.BlockSpec((tm, tn), lambda i, j: (i, j))
    in_specs = [pl.BlockSpec((tm, k), lambda i, j: (i, 0)), pl.BlockSpec((k, tn), lambda i, j: (0, j))]
    args = (a, w)
    body = functools.partial(_mm_kernel, act=act)
    if residual is not None:
        assert act is None and out_dtype == F32
        est += 2 * tm * tn * 4
        in_specs.append(tile)
        args = (a, w, residual)
        body = functools.partial(_mm_res_kernel, alpha=alpha)
    return pl.pallas_call(
        body,
        grid=(m // tm, n // tn),
        in_specs=in_specs,
        out_specs=tile,
        out_shape=jax.ShapeDtypeStruct((m, n), out_dtype),
        compiler_params=_params(("parallel", "parallel"), est),
        name="matmul_" + (act or ("residual" if residual is not None else "plain")),
    )(*args)


def _mm_acc_kernel(a_ref, w_ref, r_ref, o_ref, acc_ref, *, alpha):
    kk = pl.program_id(2)

    @pl.when(kk == 0)
    def _():
        acc_ref[...] = jnp.zeros_like(acc_ref)

    acc_ref[...] += jnp.dot(a_ref[...], w_ref[...], preferred_element_type=F32)

    @pl.when(kk == pl.num_programs(2) - 1)
    def _():
        o_ref[...] = alpha * r_ref[...] + acc_ref[...]


def _matmul_ktiled_residual(a, w, residual, alpha, tm_cap=1024, tn_cap=2048, tk_cap=1024):
    m, k = a.shape
    n = w.shape[1]
    tm = _pick_tile(m, tm_cap)
    tn = _pick_tile(n, tn_cap)
    tk = _pick_tile(k, tk_cap)
    est = 2 * (tm * tk * 2 + tk * tn * 2 + 2 * tm * tn * 4) + 2 * tm * tn * 4
    tile = pl.BlockSpec((tm, tn), lambda i, j, kk: (i, j))
    return pl.pallas_call(
        functools.partial(_mm_acc_kernel, alpha=alpha),
        grid=(m // tm, n // tn, k // tk),
        in_specs=[pl.BlockSpec((tm, tk), lambda i, j, kk: (i, kk)),
                  pl.BlockSpec((tk, tn), lambda i, j, kk: (kk, j)),
                  tile],
        out_specs=tile,
        out_shape=jax.ShapeDtypeStruct((m, n), F32),
        scratch_shapes=[pltpu.VMEM((tm, tn), F32)],
        compiler_params=_params(("parallel", "parallel", "arbitrary"), est),
        name="matmul_ktiled_residual",
    )(a, w, residual)


def _ln_kernel(z_ref, g_ref, b_ref, o_ref, ob_ref, *, eps):
    z = z_ref[...]
    mu = jnp.mean(z, axis=-1, keepdims=True)
    zc = z - mu
    var = jnp.mean(zc * zc, axis=-1, keepdims=True)
    out = zc * lax.rsqrt(var + eps) * g_ref[...] + b_ref[...]
    o_ref[...] = out
    ob_ref[...] = out.astype(BF16)


def _layernorm(z, g, b):
    m, d = z.shape
    tm = _pick_tile(m, 256, SUBLANE)
    est = 2 * tm * d * (4 + 4 + 2) + 4 * tm * d * 4
    row = pl.BlockSpec((tm, d), lambda i: (i, 0))
    vec = pl.BlockSpec((1, d), lambda i: (0, 0))
    return pl.pallas_call(
        functools.partial(_ln_kernel, eps=LN_EPS),
        grid=(m // tm,),
        in_specs=[row, vec, vec],
        out_specs=[row, row],
        out_shape=[jax.ShapeDtypeStruct((m, d), F32), jax.ShapeDtypeStruct((m, d), BF16)],
        compiler_params=_params(("parallel",), est),
        name="layernorm",
    )(z, g.reshape(1, d), b.reshape(1, d))


def _gates_kernel(x_ref, w_ref, bias_ref, tri_ref, aux_ref, ct_ref, carry_ref, *, n_fox):
    blk = pl.program_id(1)

    @pl.when(blk == 0)
    def _():
        carry_ref[...] = jnp.zeros_like(carry_ref)

    s = jnp.dot(x_ref[...], w_ref[...], preferred_element_type=F32) + bias_ref[...]
    soft = jnp.log(1.0 + jnp.exp(-jnp.abs(s)))
    log_sig = jnp.minimum(s, 0.0) - soft
    softplus = jnp.maximum(s, 0.0) + soft
    csum = carry_ref[...] + _dot_sel_lhs(tri_ref[...], log_sig)
    carry_ref[...] = csum[-1:, :]
    lane = lax.broadcasted_iota(jnp.int32, s.shape, 1)
    aux = jnp.where(lane < n_fox, csum, softplus)
    aux_ref[...] = aux
    ct_ref[0] = aux.T[:SUBLANE, :]


def _small_gates(x_bf, w_small, bias_small, bsz, s_len, n_fox):
    m, d = x_bf.shape
    t = _pick_tile(s_len, 512, LANE)
    nblk = s_len // t
    tri = jnp.asarray(np.tril(np.ones((t, t), np.float32)), BF16)
    est = 2 * (t * d * 2 + d * LANE * 2 + t * t * 2 + t * LANE * 4 * 2) + 8 * t * LANE * 4
    return pl.pallas_call(
        functools.partial(_gates_kernel, n_fox=n_fox),
        grid=(bsz, nblk),
        in_specs=[pl.BlockSpec((t, d), lambda b, i: (b * nblk + i, 0)),
                  pl.BlockSpec((d, LANE), lambda b, i: (0, 0)),
                  pl.BlockSpec((1, LANE), lambda b, i: (0, 0)),
                  pl.BlockSpec((t, t), lambda b, i: (0, 0))],
        out_specs=[pl.BlockSpec((t, LANE), lambda b, i: (b * nblk + i, 0)),
                   pl.BlockSpec((1, SUBLANE, t), lambda b, i: (b, 0, i))],
        out_shape=[jax.ShapeDtypeStruct((m, LANE), F32),
                   jax.ShapeDtypeStruct((bsz, SUBLANE, s_len), F32)],
        scratch_shapes=[pltpu.VMEM((1, LANE), F32)],
        compiler_params=_params(("parallel", "arbitrary"), est),
        name="small_gates",
    )(x_bf, w_small, bias_small, tri)


FOX_STRIP = 64
FOX_PAIR = MXU_DIM // FOX_HEAD_DIM
LOG2E = 1.4426950408889634


def _fox_kernel(q_ref, k_ref, v_ref, ct_ref, o_ref, vaug_ref, m_ref, acc_ref, p_ref, *, tq, scale):
    hp = pl.program_id(1)
    i = pl.program_id(2)
    D = FOX_HEAD_DIM
    k1 = scale * LOG2E
    nstrip = tq // FOX_STRIP
    s_len = k_ref.shape[0]

    @pl.when(i == 0)
    def _():
        ones_col = (lax.broadcasted_iota(jnp.int32, (s_len, D), 1) == 0).astype(BF16)
        for hh in range(FOX_PAIR):
            vaug_ref[hh, :, 0:D] = v_ref[:, hh * D:(hh + 1) * D]
            vaug_ref[hh, :, D:2 * D] = ones_col

    def block(j, masked):
        ks = pl.multiple_of(j * tq, tq)
        for hh in range(FOX_PAIR):
            hs = slice(hh * D, (hh + 1) * D)
            c2 = ct_ref[0, pl.ds(hp * FOX_PAIR + hh, 1), pl.ds(ks, tq)] * LOG2E
            raw = _dot_nt(q_ref[:, hs], k_ref[pl.ds(ks, tq), hs])
            for r in range(nstrip):
                rs = slice(r * FOX_STRIP, (r + 1) * FOX_STRIP)
                s = raw[rs, :] * k1 - c2
                if masked:
                    row = lax.broadcasted_iota(jnp.int32, (FOX_STRIP, tq), 0) + r * FOX_STRIP
                    col = lax.broadcasted_iota(jnp.int32, (FOX_STRIP, tq), 1)
                    s = jnp.where(col <= row, s, -jnp.inf)
                m_old = m_ref[hh, rs, :]
                m_new = jnp.maximum(m_old, jnp.broadcast_to(jnp.max(s, axis=1, keepdims=True), (FOX_STRIP, D)))
                corr = jnp.exp2(m_old - m_new)
                p = jnp.exp2(s - jnp.concatenate([m_new] * (tq // D), axis=1))
                m_ref[hh, rs, :] = m_new
                acc_ref[hh, rs, :] = acc_ref[hh, rs, :] * jnp.concatenate([corr, corr], axis=1)
                p_ref[hh, rs, :] = p.astype(BF16)
            acc_ref[hh] += jnp.dot(p_ref[hh], vaug_ref[hh, pl.ds(ks, tq), :], preferred_element_type=F32)

    m_ref[...] = jnp.full(m_ref.shape, -jnp.inf, F32)
    acc_ref[...] = jnp.zeros(acc_ref.shape, F32)
    block(i, True)

    def body(j, carry):
        block(j, False)
        return carry

    lax.fori_loop(0, i, body, 0)
    for hh in range(FOX_PAIR):
        acc = acc_ref[hh]
        o_ref[:, hh * D:(hh + 1) * D] = (acc[:, 0:D] / acc[:, D:D + 1]).astype(o_ref.dtype)


def _fox_attention(u, ct, bsz, s_len, mix, n_heads):
    m = u.shape[0]
    tq = _pick_tile(s_len, 512, LANE)
    nq = s_len // tq
    pw = FOX_PAIR * FOX_HEAD_DIM
    assert n_heads % FOX_PAIR == 0 and tq % FOX_STRIP == 0
    hb = mix // pw
    est = (2 * (2 * tq * pw * 2 + 2 * s_len * pw * 2 + SUBLANE * s_len * 4)
           + FOX_PAIR * (2 * s_len * FOX_HEAD_DIM * 2 + tq * LANE * 4 + tq * pw * 4 + tq * tq * 2) + 6 * tq * tq * 4)
    return pl.pallas_call(
        functools.partial(_fox_kernel, tq=tq, scale=FOX_HEAD_DIM ** -0.5),
        grid=(bsz, n_heads // FOX_PAIR, nq),
        in_specs=[pl.BlockSpec((tq, pw), lambda b, h, i: (b * nq + i, h)),
                  pl.BlockSpec((s_len, pw), lambda b, h, i: (b, hb + h)),
                  pl.BlockSpec((s_len, pw), lambda b, h, i: (b, 2 * hb + h)),
                  pl.BlockSpec((1, SUBLANE, s_len), lambda b, h, i: (b, 0, 0))],
        out_specs=pl.BlockSpec((tq, pw), lambda b, h, i: (b * nq + i, h)),
        out_shape=jax.ShapeDtypeStruct((m, mix), BF16),
        scratch_shapes=[pltpu.VMEM((FOX_PAIR, s_len, 2 * FOX_HEAD_DIM), BF16),
                        pltpu.VMEM((FOX_PAIR, tq, FOX_HEAD_DIM), F32),
                        pltpu.VMEM((FOX_PAIR, tq, 2 * FOX_HEAD_DIM), F32),
                        pltpu.VMEM((FOX_PAIR, tq, tq), BF16)],
        compiler_params=_params(("parallel", "parallel", "arbitrary"), est),
        name="fox_attention",
    )(u, u, u, ct)


def _ssd_kernel(z_ref, xbc_ref, aux_ref, cw_ref, cb_ref, aneg_ref, e_ref, tri_ref, d_ref, g_ref,
                o_ref, xext_ref, state_ref, *, mix, groups, kconv):
    L = SSD_CHUNK
    N = SSD_STATE
    GW = SSD_GROUP_WIDTH
    c = pl.program_id(1)

    @pl.when(c == 0)
    def _():
        state_ref[...] = jnp.zeros_like(state_ref)
        xext_ref[0:SUBLANE, :] = jnp.zeros((SUBLANE, xext_ref.shape[1]), F32)

    xext_ref[SUBLANE:SUBLANE + L, :] = xbc_ref[...]
    conv = cb_ref[...] + cw_ref[0:1, :] * xext_ref[pl.ds(SUBLANE - (kconv - 1), L), :]
    for j in range(1, kconv):
        conv = conv + cw_ref[j:j + 1, :] * xext_ref[pl.ds(SUBLANE - (kconv - 1) + j, L), :]
    xext_ref[0:SUBLANE, :] = xext_ref[L:L + SUBLANE, :]
    xbc = conv * _sigmoid(conv)
    xs = xbc[:, :mix]

    aux = aux_ref[...]
    dta = aux * aneg_ref[...]
    a_all = _dot_sel_lhs(tri_ref[...], dta)
    a_all_t = a_all.T
    e = e_ref[...]
    dt_exp = _dot_sel_rhs(aux, e)
    a_exp = _dot_sel_rhs(a_all, e)
    a_last = a_exp[L - 1:L, :]
    ea = jnp.exp(a_exp)
    xc = xs * dt_exp
    xd = (xc * jnp.exp(a_last - a_exp)).astype(BF16)

    row = lax.broadcasted_iota(jnp.int32, (L, L), 0)
    col = lax.broadcasted_iota(jnp.int32, (L, L), 1)
    causal = col <= row
    lane_lo = lax.broadcasted_iota(jnp.int32, (L, LANE), 1) < SSD_HEAD_DIM

    y_parts = []
    for g in range(groups):
        bg = xbc[:, mix + g * N: mix + (g + 1) * N].astype(BF16)
        cg = xbc[:, mix + (groups + g) * N: mix + (groups + g + 1) * N].astype(BF16)
        cbm = _dot_nt(cg, bg)
        st = state_ref[g]
        gsl = slice(g * GW, (g + 1) * GW)
        y_off = jnp.dot(cg, st.astype(BF16), preferred_element_type=F32) * ea[:, gsl]
        state_ref[g] = ea[L - 1:L, gsl] * st + _dot_tn(bg, xd[:, gsl])
        y_cols = []
        for pair in range(GW // LANE):
            c0 = g * GW + pair * LANE
            xcp = xc[:, c0:c0 + LANE]
            acc = None
            for half in range(LANE // SSD_HEAD_DIM):
                hh = DT_COL + c0 // SSD_HEAD_DIM + half
                seg = a_all[:, hh:hh + 1] - a_all_t[hh:hh + 1, :]
                dec = jnp.where(causal, jnp.exp(jnp.minimum(seg, 0.0)), 0.0)
                gm = (cbm * dec).astype(BF16)
                xm = jnp.where(lane_lo if half == 0 else jnp.logical_not(lane_lo), xcp, 0.0).astype(BF16)
                part = jnp.dot(gm, xm, preferred_element_type=F32)
                acc = part if acc is None else acc + part
            y_cols.append(acc)
        y_diag = jnp.concatenate(y_cols, axis=1)
        y = y_diag + y_off + d_ref[:, gsl] * xs[:, gsl]
        zg = z_ref[:, gsl]
        yz = y * (zg * _sigmoid(zg))
        ms = jnp.mean(yz * yz, axis=1, keepdims=True)
        y_parts.append(yz * lax.rsqrt(ms + RMS_EPS) * g_ref[:, gsl])
    o_ref[...] = jnp.concatenate(y_parts, axis=1).astype(o_ref.dtype)


def _ssd_mixer(u, z_col, xbc_col, aux, conv_w, conv_b, a_neg_row, d_exp, norm_g, bsz, s_len, mix, n_heads):
    m = u.shape[0]
    L = SSD_CHUNK
    groups = mix // SSD_GROUP_WIDTH
    width = mix + 2 * groups * SSD_STATE
    kconv = conv_w.shape[0]
    nchunk = s_len // L
    assert z_col % mix == 0 and xbc_col % width == 0
    z_blk = z_col // mix
    xbc_blk = xbc_col // width
    expand = np.zeros((LANE, mix), np.float32)
    for hh in range(n_heads):
        expand[DT_COL + hh, hh * SSD_HEAD_DIM:(hh + 1) * SSD_HEAD_DIM] = 1.0
    tri = np.tril(np.ones((L, L), np.float32))
    est = 2 * (L * mix * 4 + L * width * 4 + L * LANE * 4 + LANE * mix * 2 + L * mix * 2) + 24 * L * width * 4
    return pl.pallas_call(
        functools.partial(_ssd_kernel, mix=mix, groups=groups, kconv=kconv),
        grid=(bsz, nchunk),
        in_specs=[pl.BlockSpec((L, mix), lambda b, c: (b * nchunk + c, z_blk)),
                  pl.BlockSpec((L, width), lambda b, c: (b * nchunk + c, xbc_blk)),
                  pl.BlockSpec((L, LANE), lambda b, c: (b * nchunk + c, 0)),
                  pl.BlockSpec((kconv, width), lambda b, c: (0, 0)),
                  pl.BlockSpec((1, width), lambda b, c: (0, 0)),
                  pl.BlockSpec((1, LANE), lambda b, c: (0, 0)),
                  pl.BlockSpec((LANE, mix), lambda b, c: (0, 0)),
                  pl.BlockSpec((L, L), lambda b, c: (0, 0)),
                  pl.BlockSpec((1, mix), lambda b, c: (0, 0)),
                  pl.BlockSpec((1, mix), lambda b, c: (0, 0))],
        out_specs=pl.BlockSpec((L, mix), lambda b, c: (b * nchunk + c, 0)),
        out_shape=jax.ShapeDtypeStruct((m, mix), BF16),
        scratch_shapes=[pltpu.VMEM((L + SUBLANE, width), F32),
                        pltpu.VMEM((groups, SSD_STATE, SSD_GROUP_WIDTH), F32)],
        compiler_params=_params(("parallel", "arbitrary"), est),
        name="ssd_mixer",
    )(u, u, aux, conv_w, conv_b.reshape(1, width), a_neg_row, jnp.asarray(expand, BF16),
      jnp.asarray(tri, BF16), d_exp.reshape(1, mix), norm_g.reshape(1, mix))


def _hgrn_tables(blk):
    nlev = int(np.log2(blk))
    tri = np.tril(np.ones((blk, blk), np.float32))
    rows = []
    t = np.arange(blk)
    for l in range(1, HGRN_MM_LEVELS + 1):
        size = 1 << l
        mid = (t // size) * size + size // 2 - 1
        sign = np.where(t > mid, 1.0, -1.0).astype(np.float32)[:, None]
        rows.append(sign * (tri - tri[mid]))
    wlev = np.concatenate(rows, axis=0)
    th = np.arange(blk // 2)
    x = th[:, None] ^ th[None, :]
    lv = np.where(x > 0, np.floor(np.log2(np.maximum(x, 1))).astype(np.int32) + 1, 0)
    lv = np.where(th[None, :] > th[:, None], -1, lv).astype(np.int32)
    return tri, wlev, lv, nlev


def _hgrn_kernel(q_ref, f_ref, i_ref, g_ref, lb_ref, ng_ref, tri_ref, wlev_ref, lv_ref, o_ref, state_ref,
                 *, blk, nlev, n_heads):
    c = pl.program_id(1)

    @pl.when(c == 0)
    def _():
        state_ref[...] = jnp.zeros_like(state_ref)

    half = blk // 2
    top = slice(0, half)
    bot = slice(half, blk)
    pw = HGRN_PAIR * HGRN_HEAD_DIM

    def head_pair(p, carry):
        sl = pl.ds(pl.multiple_of(p * pw, pw), pw)
        hq = q_ref[:, sl]
        lb = lb_ref[:, sl]
        f = lb + (1.0 - lb) * _sigmoid(f_ref[:, sl])
        g2 = jnp.log(f) * LOG2E
        k = 1.0 - f
        q = hq * _sigmoid(hq)
        hi, mid, lo = _split3(g2)
        dot = lambda w, x: jnp.dot(w, x, preferred_element_type=F32)
        tri = tri_ref[...]
        b = dot(tri, hi) + (dot(tri, mid) + dot(tri, lo))
        wlev = wlev_ref[...]
        dlev = dot(wlev, hi) + dot(wlev, mid)
        facs = []
        for l in range(1, nlev + 1):
            if l <= HGRN_MM_LEVELS:
                dneg = dlev[(l - 1) * blk:l * blk]
            else:
                size = 1 << l
                hs2 = size // 2
                pieces = []
                for r0 in range(0, blk, size):
                    bmid = b[r0 + hs2 - 1:r0 + hs2]
                    pieces.append(bmid - b[r0:r0 + hs2])
                    pieces.append(b[r0 + hs2:r0 + size] - bmid)
                dneg = jnp.concatenate(pieces, axis=0)
            facs.append(jnp.exp2(dneg).astype(BF16))
        eb = jnp.exp2(b)
        b_last = b[blk - 1:blk, :]
        e_last = jnp.exp2(b_last)
        kdec = (k * jnp.exp2(b_last - b)).astype(BF16)
        qe = (q * eb).astype(BF16)
        qb2 = q.astype(BF16)
        kb2 = k.astype(BF16)
        hg = g_ref[:, sl]
        gate = ng_ref[:, sl] * (hg * _sigmoid(hg))
        lv = lv_ref[...]
        for hh in range(HGRN_PAIR):
            hs = slice(hh * HGRN_HEAD_DIM, (hh + 1) * HGRN_HEAD_DIM)
            hidx = p * HGRN_PAIR + hh
            osl = pl.ds(pl.multiple_of(hidx * HGRN_HEAD_DIM, HGRN_HEAD_DIM), HGRN_HEAD_DIM)
            qb = qb2[:, hs]
            kb = kb2[:, hs]
            vb = i_ref[:, osl]
            a_tl = jnp.where(lv == 0, _dot_nt(qb[top], kb[top]), 0.0)
            a_br = jnp.where(lv == 0, _dot_nt(qb[bot], kb[bot]), 0.0)
            for l in range(1, nlev):
                fl = facs[l - 1][:, hs]
                qf = qb * fl
                kf = kb * fl
                a_tl = jnp.where(lv == l, _dot_nt(qf[top], kf[top]), a_tl)
                a_br = jnp.where(lv == l, _dot_nt(qf[bot], kf[bot]), a_br)
            ft = facs[nlev - 1][:, hs]
            a_bl = _dot_nt(qb[bot] * ft[bot], kb[top] * ft[top])
            st = state_ref[hidx]
            o_top = jnp.dot(a_tl.astype(BF16), vb[top], preferred_element_type=F32)
            o_bot = (jnp.dot(a_bl.astype(BF16), vb[top], preferred_element_type=F32)
                     + jnp.dot(a_br.astype(BF16), vb[bot], preferred_element_type=F32))
            o = jnp.concatenate([o_top, o_bot], axis=0)
            o = o + _dot_nt(qe[:, hs], st.astype(BF16))
            state_ref[hidx] = e_last[:, hs] * st + _dot_tn(vb, kdec[:, hs])
            ms = jnp.mean(o * o, axis=1, keepdims=True)
            y = o * lax.rsqrt(ms + RMS_EPS) * gate[:, hs]
            o_ref[:, osl] = y.astype(o_ref.dtype)
        return carry

    lax.fori_loop(0, n_heads // HGRN_PAIR, head_pair, 0)


def _hgrn_mixer(uf, q_col, f_col, g_col, ub, i_col, lb, norm_g, bsz, s_len, mix):
    m = uf.shape[0]
    blk = _pick_tile(s_len, HGRN_BLOCK, LANE)
    n_heads = mix // HGRN_HEAD_DIM
    tri, wlev, lv, nlev = _hgrn_tables(blk)
    assert nlev > HGRN_MM_LEVELS and n_heads % HGRN_PAIR == 0
    nblk = s_len // blk
    spec = lambda col: pl.BlockSpec((blk, mix), lambda b, c, k=col // mix: (b * nblk + c, k))
    vec = pl.BlockSpec((1, mix), lambda b, c: (0, 0))
    pw = HGRN_PAIR * HGRN_HEAD_DIM
    est = (2 * (3 * blk * mix * 4 + 2 * blk * mix * 2 + (tri.size + wlev.size) * 2 + lv.size * 4)
           + (2 * wlev.shape[0] + (nlev + 16) * blk) * pw * 4 + 12 * blk * blk * 4)
    return pl.pallas_call(
        functools.partial(_hgrn_kernel, blk=blk, nlev=nlev, n_heads=n_heads),
        grid=(bsz, nblk),
        in_specs=[spec(q_col), spec(f_col), spec(i_col), spec(g_col), vec, vec,
                  pl.BlockSpec(tri.shape, lambda b, c: (0, 0)),
                  pl.BlockSpec(wlev.shape, lambda b, c: (0, 0)),
                  pl.BlockSpec(lv.shape, lambda b, c: (0, 0))],
        out_specs=pl.BlockSpec((blk, mix), lambda b, c: (b * nblk + c, 0)),
        out_shape=jax.ShapeDtypeStruct((m, mix), BF16),
        scratch_shapes=[pltpu.VMEM((n_heads, HGRN_HEAD_DIM, HGRN_HEAD_DIM), F32)],
        compiler_params=_params(("parallel", "arbitrary"), est),
        name="hgrn2_mixer",
    )(uf, uf, ub, uf, lb.reshape(1, mix), norm_g.reshape(1, mix), jnp.asarray(tri, BF16),
      jnp.asarray(wlev, BF16), jnp.asarray(lv))


def _merge_kernel(gl_ref, ya_ref, yb_ref, yc_ref, wg_ref, bg_ref, wb_ref, o_ref):
    gl = gl_ref[...]
    acc = None
    for i, y_ref in enumerate((ya_ref, yb_ref, yc_ref)):
        gate = _sigmoid(jnp.dot(gl, wg_ref[i], preferred_element_type=F32) + bg_ref[i])
        contrib = gate * jnp.dot(y_ref[...], wb_ref[i], preferred_element_type=F32)
        acc = contrib if acc is None else acc + contrib
    o_ref[...] = acc.astype(o_ref.dtype)


def _gated_merge(u, gate_col, rank, ya, yb, yc, w_gate, b_gate, w_branch):
    m = u.shape[0]
    mix = ya.shape[1]
    nbr, _, d = w_gate.shape
    tm = _pick_tile(m, 1024)
    tn = _pick_tile(d, 512)
    ybs = pl.BlockSpec((tm, mix), lambda i, j: (i, 0))
    est = 2 * (tm * rank * 4 + 3 * tm * mix * 2 + nbr * rank * tn * 2 + nbr * mix * tn * 2 + tm * tn * 2) + 6 * tm * tn * 4
    return pl.pallas_call(
        _merge_kernel,
        grid=(m // tm, d // tn),
        in_specs=[pl.BlockSpec((tm, rank), lambda i, j: (i, gate_col // rank)),
                  ybs, ybs, ybs,
                  pl.BlockSpec((nbr, rank, tn), lambda i, j: (0, 0, j)),
                  pl.BlockSpec((nbr, 1, tn), lambda i, j: (0, 0, j)),
                  pl.BlockSpec((nbr, mix, tn), lambda i, j: (0, 0, j))],
        out_specs=pl.BlockSpec((tm, tn), lambda i, j: (i, j)),
        out_shape=jax.ShapeDtypeStruct((m, d), BF16),
        compiler_params=_params(("parallel", "parallel"), est),
        name="gated_merge",
    )(u, ya, yb, yc, w_gate, b_gate.reshape(nbr, 1, d), w_branch)


def kernel(x, w_in, fox_f_bias, ssd_conv_w, ssd_conv_b, ssd_dt_bias, ssd_a_log, ssd_d, ssd_norm_g, hgrn_lower_bound, hgrn_norm_g, w_gate, b_gate, w_branch, w_out, ln1_g, ln1_b, w_up, w_down, ln2_g, ln2_b):
    bsz, s_len, d = x.shape
    depth = w_in.shape[0]
    mix = d // 4
    n_fox = fox_f_bias.shape[1]
    n_ssd = ssd_dt_bias.shape[1]
    rank = w_gate.shape[2]
    conv_dim = ssd_conv_w.shape[2]
    alpha = (2 * depth) ** 0.25
    assert mix % SSD_GROUP_WIDTH == 0 and conv_dim == 2 * mix and n_fox <= DT_COL and DT_COL + n_ssd <= LANE
    assert n_fox * FOX_HEAD_DIM == mix and n_ssd * SSD_HEAD_DIM == mix
    assert s_len % SSD_CHUNK == 0 and (4 * mix) % rank == 0

    widths = (mix, mix, mix, n_fox, mix, conv_dim, n_ssd, mix, mix, mix, mix, rank)
    offs = np.concatenate([[0], np.cumsum(widths)])
    col = lambda k: slice(int(offs[k]), int(offs[k + 1]))
    bf_groups = (0, 1, 2, 9, 11)
    f32_groups = (5, 4, 7, 8, 10)
    hi_col, gate_col = 3 * mix, 4 * mix
    xbc_col, z_col, hq_col, hf_col, hg_col = 0, conv_dim, conv_dim + mix, conv_dim + 2 * mix, conv_dim + 3 * mix

    lb_all = jnp.cumsum(jax.nn.softmax(hgrn_lower_bound.astype(F32), axis=0), axis=0)
    lb_all = lb_all - lb_all[0]

    m = bsz * s_len
    xf = x.reshape(m, d).astype(F32)
    xb = xf.astype(BF16)
    pad = LANE - DT_COL - n_ssd
    for l in range(depth):
        w_bf = jnp.concatenate([w_in[l][:, col(k)] for k in bf_groups], axis=1).astype(BF16)
        w_f32 = jnp.concatenate([w_in[l][:, col(k)] for k in f32_groups], axis=1).astype(BF16)
        w_small = jnp.concatenate(
            [w_in[l][:, col(3)], jnp.zeros((d, DT_COL - n_fox), F32), w_in[l][:, col(6)], jnp.zeros((d, pad), F32)],
            axis=1).astype(BF16)
        bias_small = jnp.concatenate(
            [fox_f_bias[l], jnp.zeros((DT_COL - n_fox,), F32), ssd_dt_bias[l], jnp.zeros((pad,), F32)]).reshape(1, LANE)
        a_neg_row = jnp.concatenate(
            [jnp.zeros((DT_COL,), F32), -jnp.exp(ssd_a_log[l].astype(F32)), jnp.zeros((pad,), F32)]).reshape(1, LANE)

        ub = _matmul(xb, w_bf, BF16, tn_cap=768)
        uf = _matmul(xb, w_f32, F32)
        aux, ct = _small_gates(xb, w_small, bias_small, bsz, s_len, n_fox)
        y_a = _fox_attention(ub, ct, bsz, s_len, mix, n_fox)
        y_b = _ssd_mixer(uf, z_col, xbc_col, aux, ssd_conv_w[l], ssd_conv_b[l], a_neg_row,
                         jnp.repeat(ssd_d[l], SSD_HEAD_DIM), ssd_norm_g[l], bsz, s_len, mix, n_ssd)
        y_c = _hgrn_mixer(uf, hq_col, hf_col, hg_col, ub, hi_col, lb_all[l],
                          jnp.tile(hgrn_norm_g[l], mix // HGRN_HEAD_DIM), bsz, s_len, mix)
        merged = _gated_merge(ub, gate_col, rank, y_a, y_b, y_c,
                              w_gate[l].astype(BF16), b_gate[l], w_branch[l].astype(BF16))
        z1 = _matmul(merged, w_out[l].astype(BF16), F32, residual=xf, alpha=alpha)
        xf, xb = _layernorm(z1, ln1_g[l], ln1_b[l])
        hmid = _matmul(xb, w_up[l].astype(BF16), BF16, act="relu2")
        z2 = _matmul_ktiled_residual(hmid, w_down[l].astype(BF16), xf, alpha)
        xf, xb = _layernorm(z2, ln2_g[l], ln2_b[l])
    return xf.reshape(bsz, s_len, d).astype(x.dtype)
```

```python
import functools
import math

import numpy as np
import jax
import jax.numpy as jnp
from jax import lax
from jax.experimental import pallas as pl
from jax.experimental.pallas import tpu as pltpu

F32 = jnp.float32
BF16 = jnp.bfloat16

LANE = 128
SUBLANE = 8
MXU_DIM = 256
VMEM_BYTES = 64 * 1024 * 1024
VMEM_CAP = VMEM_BYTES - 6 * 1024 * 1024

FOX_HEAD_DIM = 128
SSD_HEAD_DIM = 64
SSD_STATE = 128
SSD_CHUNK = 128
SSD_GROUP_WIDTH = 256
HGRN_HEAD_DIM = 128
HGRN_BLOCK = 256
HGRN_MM_LEVELS = 3
HGRN_GROUP = 4
LN_EPS = 1e-5
RMS_EPS = 1e-6
DT_COL = 8


def _pick_tile(n, cap, quantum=MXU_DIM):
    for q in (quantum, LANE):
        best = 0
        t = q
        while t <= min(n, cap):
            if n % t == 0:
                best = t
            t += q
        if best:
            return best
    return n


def _params(semantics, vmem_estimate):
    limit = int(min(max(vmem_estimate * 5 // 4 + (2 << 20), 16 << 20), VMEM_CAP))
    return pltpu.CompilerParams(dimension_semantics=semantics, vmem_limit_bytes=limit)


def _sigmoid(x):
    return 1.0 / (1.0 + jnp.exp(-x))


def _split3(x):
    hi = x.astype(BF16)
    r = x - hi.astype(F32)
    mid = r.astype(BF16)
    lo = (r - mid.astype(F32)).astype(BF16)
    return hi, mid, lo


def _dot_sel_lhs(sel, x):
    hi, mid, lo = _split3(x)
    d = lambda p: jnp.dot(sel, p, preferred_element_type=F32)
    return d(hi) + (d(mid) + d(lo))


def _dot_sel_rhs(x, sel):
    hi, mid, lo = _split3(x)
    d = lambda p: jnp.dot(p, sel, preferred_element_type=F32)
    return d(hi) + (d(mid) + d(lo))


def _dot_nt(a, b):
    return lax.dot_general(a, b, (((1,), (1,)), ((), ())), preferred_element_type=F32)


def _dot_tn(a, b):
    return lax.dot_general(a, b, (((0,), (0,)), ((), ())), preferred_element_type=F32)


SIDE_ROWS = 16


def _side_plan(side, grid):
    stack, layer = side
    shape = stack.shape[1:]
    cols = shape[-1]
    total_rows = math.prod(shape[:-1])
    steps = grid[0] * grid[1]
    slabs = steps
    while slabs > 1 and (steps % slabs or total_rows % (slabs * SIDE_ROWS)):
        slabs -= 1
    assert total_rows % (slabs * SIDE_ROWS) == 0
    hold = steps // slabs
    rows = total_rows // slabs
    in_spec = pl.BlockSpec((rows, cols), lambda i, j: (layer * slabs + (i * grid[1] + j) // hold, 0))
    out_spec = pl.BlockSpec((rows, cols), lambda i, j: ((i * grid[1] + j) // hold, 0))
    out_shape = jax.ShapeDtypeStruct((total_rows, cols), BF16)
    return stack.reshape(stack.shape[0] * total_rows, cols), in_spec, out_spec, out_shape, rows * cols, shape


def _mm_kernel(a_ref, w_ref, *refs, act, has_side):
    o_ref = refs[1] if has_side else refs[0]
    acc = jnp.dot(a_ref[...], w_ref[...], preferred_element_type=F32)
    if act == "relu2":
        acc = jnp.square(jnp.maximum(acc, 0.0))
    o_ref[...] = acc.astype(o_ref.dtype)
    if has_side:
        refs[2][...] = refs[0][...].astype(BF16)


def _mm_res_kernel(a_ref, w_ref, r_ref, o_ref, *, alpha):
    acc = jnp.dot(a_ref[...], w_ref[...], preferred_element_type=F32)
    o_ref[...] = alpha * r_ref[...] + acc


def _matmul(a, w, out_dtype, act=None, residual=None, alpha=None, side=None, layer=None, tm_cap=1024, tn_cap=1024):
    m, k = a.shape
    n = w.shape[-1]
    tm = _pick_tile(m, tm_cap)
    tn = _pick_tile(n, tn_cap)
    grid = (m // tm, n // tn)
    osz = jnp.dtype(out_dtype).itemsize
    est = 2 * (tm * k * 2 + k * tn * 2 + tm * tn * osz) + tm * tn * 4
    tile = pl.BlockSpec((tm, tn), lambda i, j: (i, j))
    w_spec = (pl.BlockSpec((k, tn), lambda i, j: (0, j)) if layer is None
              else pl.BlockSpec((None, k, tn), lambda i, j: (layer, 0, j)))
    in_specs = [pl.BlockSpec((tm, k), lambda i, j: (i, 0)), w_spec]
    args = (a, w)
    out_specs = tile
    out_shape = jax.ShapeDtypeStruct((m, n), out_dtype)
    body = functools.partial(_mm_kernel, act=act, has_side=side is not None)
    if residual is not None:
        assert act is None and out_dtype == F32 and side is None
        est += 2 * tm * tn * 4
        in_specs.append(tile)
        args = (a, w, residual)
        body = functools.partial(_mm_res_kernel, alpha=alpha)
    if side is not None:
        view, s_in, s_out, sshape, selems, side_shape = _side_plan(side, grid)
        est += 2 * selems * (4 + 2)
        in_specs.append(s_in)
        args = (a, w, view)
        out_specs = [tile, s_out]
        out_shape = [out_shape, sshape]
    res = pl.pallas_call(
        body,
        grid=grid,
        in_specs=in_specs,
        out_specs=out_specs,
        out_shape=out_shape,
        compiler_params=_params(("parallel", "parallel"), est),
        name="matmul_" + (act or ("residual" if residual is not None else "plain")),
    )(*args)
    if side is not None:
        return res[0], res[1].reshape(side_shape)
    return res


def _mm_acc_kernel(a_ref, w_ref, r_ref, o_ref, acc_ref, *, alpha):
    kk = pl.program_id(2)

    @pl.when(kk == 0)
    def _():
        acc_ref[...] = jnp.zeros_like(acc_ref)

    acc_ref[...] += jnp.dot(a_ref[...], w_ref[...], preferred_element_type=F32)

    @pl.when(kk == pl.num_programs(2) - 1)
    def _():
        o_ref[...] = alpha * r_ref[...] + acc_ref[...]


def _matmul_ktiled_residual(a, w, residual, alpha, tm_cap=1024, tn_cap=2048, tk_cap=1024):
    m, k = a.shape
    n = w.shape[1]
    tm = _pick_tile(m, tm_cap)
    tn = _pick_tile(n, tn_cap)
    tk = _pick_tile(k, tk_cap)
    est = 2 * (tm * tk * 2 + tk * tn * 2 + 2 * tm * tn * 4) + 2 * tm * tn * 4
    tile = pl.BlockSpec((tm, tn), lambda i, j, kk: (i, j))
    return pl.pallas_call(
        functools.partial(_mm_acc_kernel, alpha=alpha),
        grid=(m // tm, n // tn, k // tk),
        in_specs=[pl.BlockSpec((tm, tk), lambda i, j, kk: (i, kk)),
                  pl.BlockSpec((tk, tn), lambda i, j, kk: (kk, j)),
                  tile],
        out_specs=tile,
        out_shape=jax.ShapeDtypeStruct((m, n), F32),
        scratch_shapes=[pltpu.VMEM((tm, tn), F32)],
        compiler_params=_params(("parallel", "parallel", "arbitrary"), est),
        name="matmul_ktiled_residual",
    )(a, w, residual)


def _regroup_kernel(w_ref, *out_refs, plans):
    for o_ref, plan in zip(out_refs, plans):
        dst = 0
        for src, width in plan:
            if src is None:
                o_ref[0, :, dst:dst + width] = jnp.zeros((o_ref.shape[1], width), o_ref.dtype)
            else:
                o_ref[0, :, dst:dst + width] = w_ref[0, :, src:src + width].astype(o_ref.dtype)
            dst += width


def _regroup_weights(w, plans):
    depth, k, n = w.shape
    tr = _pick_tile(k, 256, 16)
    widths = [sum(width for _, width in plan) for plan in plans]
    est = 2 * tr * (n * 4 + sum(widths) * 2) + tr * n * 4
    return pl.pallas_call(
        functools.partial(_regroup_kernel, plans=plans),
        grid=(depth, k // tr),
        in_specs=[pl.BlockSpec((1, tr, n), lambda l, i: (l, i, 0))],
        out_specs=[pl.BlockSpec((1, tr, wd), lambda l, i: (l, i, 0)) for wd in widths],
        out_shape=[jax.ShapeDtypeStruct((depth, k, wd), BF16) for wd in widths],
        compiler_params=_params(("parallel", "parallel"), est),
        name="regroup_weights",
    )(w)


def _ln_kernel(z_ref, g_ref, b_ref, o_ref, ob_ref, *, eps):
    z = z_ref[...]
    mu = jnp.mean(z, axis=-1, keepdims=True)
    zc = z - mu
    var = jnp.mean(zc * zc, axis=-1, keepdims=True)
    out = zc * lax.rsqrt(var + eps) * g_ref[...] + b_ref[...]
    o_ref[...] = out
    ob_ref[...] = out.astype(BF16)


def _layernorm(z, g, b):
    m, d = z.shape
    tm = _pick_tile(m, 256, SUBLANE)
    est = 2 * tm * d * (4 + 4 + 2) + 4 * tm * d * 4
    row = pl.BlockSpec((tm, d), lambda i: (i, 0))
    vec = pl.BlockSpec((1, d), lambda i: (0, 0))
    return pl.pallas_call(
        functools.partial(_ln_kernel, eps=LN_EPS),
        grid=(m // tm,),
        in_specs=[row, vec, vec],
        out_specs=[row, row],
        out_shape=[jax.ShapeDtypeStruct((m, d), F32), jax.ShapeDtypeStruct((m, d), BF16)],
        compiler_params=_params(("parallel",), est),
        name="layernorm",
    )(z, g.reshape(1, d), b.reshape(1, d))


def _gates_kernel(x_ref, w_ref, bias_ref, tri_ref, aux_ref, ct_ref, carry_ref, *, n_fox):
    blk = pl.program_id(1)

    @pl.when(blk == 0)
    def _():
        carry_ref[...] = jnp.zeros_like(carry_ref)

    s = jnp.dot(x_ref[...], w_ref[...], preferred_element_type=F32) + bias_ref[...]
    soft = jnp.log(1.0 + jnp.exp(-jnp.abs(s)))
    log_sig = jnp.minimum(s, 0.0) - soft
    softplus = jnp.maximum(s, 0.0) + soft
    csum = carry_ref[...] + _dot_sel_lhs(tri_ref[...], log_sig)
    carry_ref[...] = csum[-1:, :]
    lane = lax.broadcasted_iota(jnp.int32, s.shape, 1)
    aux = jnp.where(lane < n_fox, csum, softplus)
    aux_ref[...] = aux
    ct_ref[0] = aux.T[:SUBLANE, :]


def _small_gates(x_bf, w_small, layer, bias_small, bsz, s_len, n_fox):
    m, d = x_bf.shape
    t = _pick_tile(s_len, 512, LANE)
    nblk = s_len // t
    tri = jnp.asarray(np.tril(np.ones((t, t), np.float32)), BF16)
    est = 2 * (t * d * 2 + d * LANE * 2 + t * t * 2 + t * LANE * 4 * 2) + 8 * t * LANE * 4
    return pl.pallas_call(
        functools.partial(_gates_kernel, n_fox=n_fox),
        grid=(bsz, nblk),
        in_specs=[pl.BlockSpec((t, d), lambda b, i: (b * nblk + i, 0)),
                  pl.BlockSpec((None, d, LANE), lambda b, i: (layer, 0, 0)),
                  pl.BlockSpec((1, LANE), lambda b, i: (0, 0)),
                  pl.BlockSpec((t, t), lambda b, i: (0, 0))],
        out_specs=[pl.BlockSpec((t, LANE), lambda b, i: (b * nblk + i, 0)),
                   pl.BlockSpec((1, SUBLANE, t), lambda b, i: (b, 0, i))],
        out_shape=[jax.ShapeDtypeStruct((m, LANE), F32),
                   jax.ShapeDtypeStruct((bsz, SUBLANE, s_len), F32)],
        scratch_shapes=[pltpu.VMEM((1, LANE), F32)],
        compiler_params=_params(("parallel", "arbitrary"), est),
        name="small_gates",
    )(x_bf, w_small, bias_small, tri)


FOX_STRIP = 64
FOX_GROUP = 4
LOG2E = 1.4426950408889634


def _fox_kernel(q_ref, k_ref, v_ref, ct_ref, o_ref, vaug_ref, m_ref, acc_ref, p_ref, *, tq, scale):
    hp = pl.program_id(1)
    i = pl.program_id(2)
    D = FOX_HEAD_DIM
    grp = vaug_ref.shape[0]
    k1 = scale * LOG2E
    nstrip = tq // FOX_STRIP
    s_len = k_ref.shape[0]

    @pl.when(i == 0)
    def _():
        ones_col = (lax.broadcasted_iota(jnp.int32, (s_len, D), 1) == 0).astype(BF16)
        for hh in range(grp):
            vaug_ref[hh, :, 0:D] = v_ref[:, hh * D:(hh + 1) * D]
            vaug_ref[hh, :, D:2 * D] = ones_col

    def block(j, masked):
        ks = pl.multiple_of(j * tq, tq)
        for hh in range(grp):
            hs = slice(hh * D, (hh + 1) * D)
            c2 = ct_ref[0, pl.ds(hp * grp + hh, 1), pl.ds(ks, tq)] * LOG2E
            raw = _dot_nt(q_ref[:, hs], k_ref[pl.ds(ks, tq), hs])
            for r in range(nstrip):
                rs = slice(r * FOX_STRIP, (r + 1) * FOX_STRIP)
                cw = min(tq, -(-((r + 1) * FOX_STRIP) // D) * D) if masked else tq
                s = raw[rs, :cw] * k1 - c2[:, :cw]
                if masked:
                    row = lax.broadcasted_iota(jnp.int32, (FOX_STRIP, cw), 0) + r * FOX_STRIP
                    col = lax.broadcasted_iota(jnp.int32, (FOX_STRIP, cw), 1)
                    s = jnp.where(col <= row, s, -jnp.inf)
                m_new = jnp.broadcast_to(jnp.max(s, axis=1, keepdims=True), (FOX_STRIP, D))
                if not masked:
                    m_old = m_ref[hh, rs, :]
                    m_new = jnp.maximum(m_old, m_new)
                    corr = jnp.exp2(m_old - m_new)
                    acc_ref[hh, rs, :] = acc_ref[hh, rs, :] * jnp.concatenate([corr, corr], axis=1)
                p = jnp.exp2(s - jnp.concatenate([m_new] * (cw // D), axis=1))
                m_ref[hh, rs, :] = m_new
                p_ref[hh, rs, :cw] = p.astype(BF16)
                if cw < tq:
                    p_ref[hh, rs, cw:] = jnp.zeros((FOX_STRIP, tq - cw), BF16)
            pv = jnp.dot(p_ref[hh], vaug_ref[hh, pl.ds(ks, tq), :], preferred_element_type=F32)
            if masked:
                acc_ref[hh] = pv
            else:
                acc_ref[hh] += pv

    block(i, True)

    def body(j, carry):
        block(j, False)
        return carry

    lax.fori_loop(0, i, body, 0)
    for hh in range(grp):
        acc = acc_ref[hh]
        o_ref[:, hh * D:(hh + 1) * D] = (acc[:, 0:D] / acc[:, D:D + 1]).astype(o_ref.dtype)


def _fox_attention(u, ct, bsz, s_len, mix, n_heads):
    m = u.shape[0]
    tq = _pick_tile(s_len, 512, LANE)
    nq = s_len // tq
    grp = math.gcd(n_heads, FOX_GROUP)
    pw = grp * FOX_HEAD_DIM
    assert tq % FOX_STRIP == 0
    hb = mix // pw
    est = (2 * (2 * tq * pw * 2 + 2 * s_len * pw * 2 + SUBLANE * s_len * 4)
           + grp * (2 * s_len * FOX_HEAD_DIM * 2 + tq * LANE * 4 + tq * pw * 4 + tq * tq * 2) + 6 * tq * tq * 4)
    return pl.pallas_call(
        functools.partial(_fox_kernel, tq=tq, scale=FOX_HEAD_DIM ** -0.5),
        grid=(bsz, n_heads // grp, nq),
        in_specs=[pl.BlockSpec((tq, pw), lambda b, h, i: (b * nq + i, h)),
                  pl.BlockSpec((s_len, pw), lambda b, h, i: (b, hb + h)),
                  pl.BlockSpec((s_len, pw), lambda b, h, i: (b, 2 * hb + h)),
                  pl.BlockSpec((1, SUBLANE, s_len), lambda b, h, i: (b, 0, 0))],
        out_specs=pl.BlockSpec((tq, pw), lambda b, h, i: (b * nq + i, h)),
        out_shape=jax.ShapeDtypeStruct((m, mix), BF16),
        scratch_shapes=[pltpu.VMEM((grp, s_len, 2 * FOX_HEAD_DIM), BF16),
                        pltpu.VMEM((grp, tq, FOX_HEAD_DIM), F32),
                        pltpu.VMEM((grp, tq, 2 * FOX_HEAD_DIM), F32),
                        pltpu.VMEM((grp, tq, tq), BF16)],
        compiler_params=_params(("parallel", "parallel", "arbitrary"), est),
        name="fox_attention",
    )(u, u, u, ct)


def _ssd_kernel(z_ref, xbc_ref, aux_ref, cw_ref, cb_ref, aneg_ref, e_ref, tri_ref, d_ref, g_ref,
                *refs, mix, groups, kconv, n_side):
    side_in = refs[:n_side]
    o_ref = refs[n_side]
    side_out = refs[n_side + 1:2 * n_side + 1]
    xext_ref, state_ref = refs[2 * n_side + 1:]
    for s_in, s_out in zip(side_in, side_out):
        s_out[...] = s_in[...].astype(BF16)
    L = SSD_CHUNK
    N = SSD_STATE
    GW = SSD_GROUP_WIDTH
    c = pl.program_id(1)

    @pl.when(c == 0)
    def _():
        state_ref[...] = jnp.zeros_like(state_ref)
        xext_ref[0:SUBLANE, :] = jnp.zeros((SUBLANE, xext_ref.shape[1]), F32)

    xext_ref[SUBLANE:SUBLANE + L, :] = xbc_ref[...]
    conv = cb_ref[...] + cw_ref[0:1, :] * xext_ref[pl.ds(SUBLANE - (kconv - 1), L), :]
    for j in range(1, kconv):
        conv = conv + cw_ref[j:j + 1, :] * xext_ref[pl.ds(SUBLANE - (kconv - 1) + j, L), :]
    xext_ref[0:SUBLANE, :] = xext_ref[L:L + SUBLANE, :]
    xbc = conv * _sigmoid(conv)
    xs = xbc[:, :mix]

    aux = aux_ref[...]
    dta = aux * aneg_ref[...]
    a_all = _dot_sel_lhs(tri_ref[...], dta)
    a_all_t = a_all.T
    e = e_ref[...]
    dt_exp = _dot_sel_rhs(aux, e)
    a_exp = _dot_sel_rhs(a_all, e)
    a_last = a_exp[L - 1:L, :]
    ea = jnp.exp(a_exp)
    xc = xs * dt_exp
    xd = (xc * jnp.exp(a_last - a_exp)).astype(BF16)

    row = lax.broadcasted_iota(jnp.int32, (L, L), 0)
    col = lax.broadcasted_iota(jnp.int32, (L, L), 1)
    causal = col <= row
    lane_lo = lax.broadcasted_iota(jnp.int32, (L, LANE), 1) < SSD_HEAD_DIM

    y_parts = []
    for g in range(groups):
        bg = xbc[:, mix + g * N: mix + (g + 1) * N].astype(BF16)
        cg = xbc[:, mix + (groups + g) * N: mix + (groups + g + 1) * N].astype(BF16)
        cbm = _dot_nt(cg, bg)
        st = state_ref[g]
        gsl = slice(g * GW, (g + 1) * GW)
        y_off = jnp.dot(cg, st.astype(BF16), preferred_element_type=F32) * ea[:, gsl]
        state_ref[g] = ea[L - 1:L, gsl] * st + _dot_tn(bg, xd[:, gsl])
        y_cols = []
        for pair in range(GW // LANE):
            c0 = g * GW + pair * LANE
            xcp = xc[:, c0:c0 + LANE]
            acc = None
            for half in range(LANE // SSD_HEAD_DIM):
                hh = DT_COL + c0 // SSD_HEAD_DIM + half
                seg = a_all[:, hh:hh + 1] - a_all_t[hh:hh + 1, :]
                dec = jnp.where(causal, jnp.exp(jnp.minimum(seg, 0.0)), 0.0)
                gm = (cbm * dec).astype(BF16)
                xm = jnp.where(lane_lo if half == 0 else jnp.logical_not(lane_lo), xcp, 0.0).astype(BF16)
                part = jnp.dot(gm, xm, preferred_element_type=F32)
                acc = part if acc is None else acc + part
            y_cols.append(acc)
        y_diag = jnp.concatenate(y_cols, axis=1)
        y = y_diag + y_off + d_ref[:, gsl] * xs[:, gsl]
        zg = z_ref[:, gsl]
        yz = y * (zg * _sigmoid(zg))
        ms = jnp.mean(yz * yz, axis=1, keepdims=True)
        y_parts.append(yz * lax.rsqrt(ms + RMS_EPS) * g_ref[:, gsl])
    o_ref[...] = jnp.concatenate(y_parts, axis=1).astype(o_ref.dtype)


def _ssd_mixer(u, z_col, xbc_col, aux, conv_w, conv_b, a_neg_row, d_exp, norm_g, bsz, s_len, mix, n_heads, sides):
    m = u.shape[0]
    L = SSD_CHUNK
    groups = mix // SSD_GROUP_WIDTH
    width = mix + 2 * groups * SSD_STATE
    kconv = conv_w.shape[0]
    nchunk = s_len // L
    assert z_col % mix == 0 and xbc_col % width == 0
    z_blk = z_col // mix
    xbc_blk = xbc_col // width
    expand = np.zeros((LANE, mix), np.float32)
    for hh in range(n_heads):
        expand[DT_COL + hh, hh * SSD_HEAD_DIM:(hh + 1) * SSD_HEAD_DIM] = 1.0
    tri = np.tril(np.ones((L, L), np.float32))
    est = 2 * (L * mix * 4 + L * width * 4 + L * LANE * 4 + LANE * mix * 2 + L * mix * 2) + 24 * L * width * 4
    plans = [_side_plan(side, (bsz, nchunk)) for side in sides]
    est += sum(2 * p[4] * 6 for p in plans)
    res = pl.pallas_call(
        functools.partial(_ssd_kernel, mix=mix, groups=groups, kconv=kconv, n_side=len(sides)),
        grid=(bsz, nchunk),
        in_specs=[pl.BlockSpec((L, mix), lambda b, c: (b * nchunk + c, z_blk)),
                  pl.BlockSpec((L, width), lambda b, c: (b * nchunk + c, xbc_blk)),
                  pl.BlockSpec((L, LANE), lambda b, c: (b * nchunk + c, 0)),
                  pl.BlockSpec((kconv, width), lambda b, c: (0, 0)),
                  pl.BlockSpec((1, width), lambda b, c: (0, 0)),
                  pl.BlockSpec((1, LANE), lambda b, c: (0, 0)),
                  pl.BlockSpec((LANE, mix), lambda b, c: (0, 0)),
                  pl.BlockSpec((L, L), lambda b, c: (0, 0)),
                  pl.BlockSpec((1, mix), lambda b, c: (0, 0)),
                  pl.BlockSpec((1, mix), lambda b, c: (0, 0))] + [p[1] for p in plans],
        out_specs=[pl.BlockSpec((L, mix), lambda b, c: (b * nchunk + c, 0))] + [p[2] for p in plans],
        out_shape=[jax.ShapeDtypeStruct((m, mix), BF16)] + [p[3] for p in plans],
        scratch_shapes=[pltpu.VMEM((L + SUBLANE, width), F32),
                        pltpu.VMEM((groups, SSD_STATE, SSD_GROUP_WIDTH), F32)],
        compiler_params=_params(("parallel", "arbitrary"), est),
        name="ssd_mixer",
    )(u, u, aux, conv_w, conv_b.reshape(1, width), a_neg_row, jnp.asarray(expand, BF16),
      jnp.asarray(tri, BF16), d_exp.reshape(1, mix), norm_g.reshape(1, mix), *[p[0] for p in plans])
    return res[0], [r.reshape(p[5]) for r, p in zip(res[1:], plans)]


def _hgrn_tables(blk):
    nlev = int(np.log2(blk))
    tri = np.tril(np.ones((blk, blk), np.float32))
    rows = []
    t = np.arange(blk)
    for l in range(1, HGRN_MM_LEVELS + 1):
        size = 1 << l
        mid = (t // size) * size + size // 2 - 1
        sign = np.where(t > mid, 1.0, -1.0).astype(np.float32)[:, None]
        rows.append(sign * (tri - tri[mid]))
    wlev = np.concatenate(rows, axis=0)
    th = np.arange(blk // 2)
    x = th[:, None] ^ th[None, :]
    lv = np.where(x > 0, np.floor(np.log2(np.maximum(x, 1))).astype(np.int32) + 1, 0)
    lv = np.where(th[None, :] > th[:, None], -1, lv).astype(np.int32)
    return tri, wlev, lv, nlev


def _hgrn_kernel(q_ref, f_ref, i_ref, g_ref, lb_ref, ng_ref, tri_ref, wlev_ref, lv_ref, o_ref, state_ref,
                 *, blk, nlev, n_heads, grp):
    c = pl.program_id(1)

    @pl.when(c == 0)
    def _():
        state_ref[...] = jnp.zeros_like(state_ref)

    half = blk // 2
    top = slice(0, half)
    bot = slice(half, blk)
    pw = grp * HGRN_HEAD_DIM

    def head_pair(p, carry):
        sl = pl.ds(pl.multiple_of(p * pw, pw), pw)
        hq = q_ref[:, sl]
        lb = lb_ref[:, sl]
        f = lb + (1.0 - lb) * _sigmoid(f_ref[:, sl])
        g2 = jnp.log(f) * LOG2E
        k = 1.0 - f
        q = hq * _sigmoid(hq)
        hi, mid, lo = _split3(g2)
        dot = lambda w, x: jnp.dot(w, x, preferred_element_type=F32)
        tri = tri_ref[...]
        b = dot(tri, hi) + (dot(tri, mid) + dot(tri, lo))
        wlev = wlev_ref[...]
        dlev = dot(wlev, hi) + dot(wlev, mid)
        facs = []
        for l in range(1, nlev + 1):
            if l <= HGRN_MM_LEVELS:
                dneg = dlev[(l - 1) * blk:l * blk]
            else:
                size = 1 << l
                hs2 = size // 2
                pieces = []
                for r0 in range(0, blk, size):
                    bmid = b[r0 + hs2 - 1:r0 + hs2]
                    pieces.append(bmid - b[r0:r0 + hs2])
                    pieces.append(b[r0 + hs2:r0 + size] - bmid)
                dneg = jnp.concatenate(pieces, axis=0)
            facs.append(jnp.exp2(dneg).astype(BF16))
        eb = jnp.exp2(b)
        b_last = b[blk - 1:blk, :]
        e_last = jnp.exp2(b_last)
        kdec = (k * jnp.exp2(b_last - b)).astype(BF16)
        qe = (q * eb).astype(BF16)
        qb2 = q.astype(BF16)
        kb2 = k.astype(BF16)
        hg = g_ref[:, sl]
        gate = ng_ref[:, sl] * (hg * _sigmoid(hg))
        lv = lv_ref[...]
        for hh in range(grp):
            hs = slice(hh * HGRN_HEAD_DIM, (hh + 1) * HGRN_HEAD_DIM)
            hidx = p * grp + hh
            osl = pl.ds(pl.multiple_of(hidx * HGRN_HEAD_DIM, HGRN_HEAD_DIM), HGRN_HEAD_DIM)
            qb = qb2[:, hs]
            kb = kb2[:, hs]
            vb = i_ref[:, osl]
            a_tl = jnp.where(lv == 0, _dot_nt(qb[top], kb[top]), 0.0)
            a_br = jnp.where(lv == 0, _dot_nt(qb[bot], kb[bot]), 0.0)
            for l in range(1, nlev):
                fl = facs[l - 1][:, hs]
                qf = qb * fl
                kf = kb * fl
                a_tl = jnp.where(lv == l, _dot_nt(qf[top], kf[top]), a_tl)
                a_br = jnp.where(lv == l, _dot_nt(qf[bot], kf[bot]), a_br)
            ft = facs[nlev - 1][:, hs]
            a_bl = _dot_nt(qb[bot] * ft[bot], kb[top] * ft[top])
            st = state_ref[hidx]
            o_top = jnp.dot(a_tl.astype(BF16), vb[top], preferred_element_type=F32)
            o_bot = (jnp.dot(a_bl.astype(BF16), vb[top], preferred_element_type=F32)
                     + jnp.dot(a_br.astype(BF16), vb[bot], preferred_element_type=F32))
            o = jnp.concatenate([o_top, o_bot], axis=0)
            o = o + _dot_nt(qe[:, hs], st.astype(BF16))
            state_ref[hidx] = e_last[:, hs] * st + _dot_tn(vb, kdec[:, hs])
            ms = jnp.mean(o * o, axis=1, keepdims=True)
            y = o * lax.rsqrt(ms + RMS_EPS) * gate[:, hs]
            o_ref[:, osl] = y.astype(o_ref.dtype)
        return carry

    lax.fori_loop(0, n_heads // grp, head_pair, 0)


def _hgrn_mixer(uf, q_col, f_col, g_col, ub, i_col, lb, norm_g, bsz, s_len, mix):
    m = uf.shape[0]
    blk = _pick_tile(s_len, HGRN_BLOCK, LANE)
    n_heads = mix // HGRN_HEAD_DIM
    tri, wlev, lv, nlev = _hgrn_tables(blk)
    assert nlev > HGRN_MM_LEVELS
    grp = math.gcd(n_heads, HGRN_GROUP)
    nblk = s_len // blk
    spec = lambda col: pl.BlockSpec((blk, mix), lambda b, c, k=col // mix: (b * nblk + c, k))
    vec = pl.BlockSpec((1, mix), lambda b, c: (0, 0))
    pw = grp * HGRN_HEAD_DIM
    est = (2 * (3 * blk * mix * 4 + 2 * blk * mix * 2 + (tri.size + wlev.size) * 2 + lv.size * 4)
           + (2 * wlev.shape[0] + (nlev + 16) * blk) * pw * 4 + 12 * blk * blk * 4)
    return pl.pallas_call(
        functools.partial(_hgrn_kernel, blk=blk, nlev=nlev, n_heads=n_heads, grp=grp),
        grid=(bsz, nblk),
        in_specs=[spec(q_col), spec(f_col), spec(i_col), spec(g_col), vec, vec,
                  pl.BlockSpec(tri.shape, lambda b, c: (0, 0)),
                  pl.BlockSpec(wlev.shape, lambda b, c: (0, 0)),
                  pl.BlockSpec(lv.shape, lambda b, c: (0, 0))],
        out_specs=pl.BlockSpec((blk, mix), lambda b, c: (b * nblk + c, 0)),
        out_shape=jax.ShapeDtypeStruct((m, mix), BF16),
        scratch_shapes=[pltpu.VMEM((n_heads, HGRN_HEAD_DIM, HGRN_HEAD_DIM), F32)],
        compiler_params=_params(("parallel", "arbitrary"), est),
        name="hgrn2_mixer",
    )(uf, uf, ub, uf, lb.reshape(1, mix), norm_g.reshape(1, mix), jnp.asarray(tri, BF16),
      jnp.asarray(wlev, BF16), jnp.asarray(lv))


def _merge_kernel(gl_ref, ya_ref, yb_ref, yc_ref, wg_ref, bg_ref, wb_ref, side_ref, o_ref, side_out_ref):
    gl = gl_ref[...]
    acc = None
    for i, y_ref in enumerate((ya_ref, yb_ref, yc_ref)):
        gate = _sigmoid(jnp.dot(gl, wg_ref[i], preferred_element_type=F32) + bg_ref[i])
        contrib = gate * jnp.dot(y_ref[...], wb_ref[i], preferred_element_type=F32)
        acc = contrib if acc is None else acc + contrib
    o_ref[...] = acc.astype(o_ref.dtype)
    side_out_ref[...] = side_ref[...].astype(BF16)


def _gated_merge(u, gate_col, rank, ya, yb, yc, w_gate, b_gate, w_branch, side):
    m = u.shape[0]
    mix = ya.shape[1]
    nbr, _, d = w_gate.shape
    tm = _pick_tile(m, 1024)
    tn = _pick_tile(d, 512)
    grid = (m // tm, d // tn)
    ybs = pl.BlockSpec((tm, mix), lambda i, j: (i, 0))
    view, s_in, s_out, sshape, selems, side_shape = _side_plan(side, grid)
    est = (2 * (tm * rank * 2 + 3 * tm * mix * 2 + nbr * rank * tn * 2 + nbr * mix * tn * 2 + tm * tn * 2 + selems * 6)
           + 6 * tm * tn * 4)
    tile = pl.BlockSpec((tm, tn), lambda i, j: (i, j))
    merged, side_bf = pl.pallas_call(
        _merge_kernel,
        grid=grid,
        in_specs=[pl.BlockSpec((tm, rank), lambda i, j: (i, gate_col // rank)),
                  ybs, ybs, ybs,
                  pl.BlockSpec((nbr, rank, tn), lambda i, j: (0, 0, j)),
                  pl.BlockSpec((nbr, 1, tn), lambda i, j: (0, 0, j)),
                  pl.BlockSpec((nbr, mix, tn), lambda i, j: (0, 0, j)),
                  s_in],
        out_specs=[tile, s_out],
        out_shape=[jax.ShapeDtypeStruct((m, d), BF16), sshape],
        compiler_params=_params(("parallel", "parallel"), est),
        name="gated_merge",
    )(u, ya, yb, yc, w_gate, b_gate.reshape(nbr, 1, d), w_branch, view)
    return merged, side_bf.reshape(side_shape)


def kernel(x, w_in, fox_f_bias, ssd_conv_w, ssd_conv_b, ssd_dt_bias, ssd_a_log, ssd_d, ssd_norm_g, hgrn_lower_bound, hgrn_norm_g, w_gate, b_gate, w_branch, w_out, ln1_g, ln1_b, w_up, w_down, ln2_g, ln2_b):
    bsz, s_len, d = x.shape
    depth = w_in.shape[0]
    mix = d // 4
    n_fox = fox_f_bias.shape[1]
    n_ssd = ssd_dt_bias.shape[1]
    rank = w_gate.shape[2]
    conv_dim = ssd_conv_w.shape[2]
    alpha = (2 * depth) ** 0.25
    assert mix % SSD_GROUP_WIDTH == 0 and conv_dim == 2 * mix and n_fox <= DT_COL and DT_COL + n_ssd <= LANE
    assert n_fox * FOX_HEAD_DIM == mix and n_ssd * SSD_HEAD_DIM == mix
    assert s_len % SSD_CHUNK == 0 and (4 * mix) % rank == 0

    widths = (mix, mix, mix, n_fox, mix, conv_dim, n_ssd, mix, mix, mix, mix, rank)
    offs = np.concatenate([[0], np.cumsum(widths)])
    col = lambda k: slice(int(offs[k]), int(offs[k + 1]))
    bf_groups = (0, 1, 2, 9, 11)
    f32_groups = (5, 4, 7, 8, 10)
    hi_col, gate_col = 3 * mix, 4 * mix
    xbc_col, z_col, hq_col, hf_col, hg_col = 0, conv_dim, conv_dim + mix, conv_dim + 2 * mix, conv_dim + 3 * mix

    lb_all = jnp.cumsum(jax.nn.softmax(hgrn_lower_bound.astype(F32), axis=0), axis=0)
    lb_all = lb_all - lb_all[0]

    m = bsz * s_len
    xf = x.reshape(m, d).astype(F32)
    xb = xf.astype(BF16)
    pad = LANE - DT_COL - n_ssd
    piece = lambda k: (int(offs[k]), int(widths[k]))
    small_plan = [piece(3)] + ([(None, DT_COL - n_fox)] if n_fox < DT_COL else []) + [piece(6)] + ([(None, pad)] if pad else [])
    w_bf_all, w_f32_all, w_small_all = _regroup_weights(
        w_in.astype(F32), [[piece(k) for k in bf_groups], [piece(k) for k in f32_groups], small_plan])
    for l in range(depth):
        bias_small = jnp.concatenate(
            [fox_f_bias[l], jnp.zeros((DT_COL - n_fox,), F32), ssd_dt_bias[l], jnp.zeros((pad,), F32)]).reshape(1, LANE)
        a_neg_row = jnp.concatenate(
            [jnp.zeros((DT_COL,), F32), -jnp.exp(ssd_a_log[l].astype(F32)), jnp.zeros((pad,), F32)]).reshape(1, LANE)

        ub = _matmul(xb, w_bf_all, BF16, layer=l, tn_cap=768)
        uf = _matmul(xb, w_f32_all, F32, layer=l)
        aux, ct = _small_gates(xb, w_small_all, l, bias_small, bsz, s_len, n_fox)
        y_a = _fox_attention(ub, ct, bsz, s_len, mix, n_fox)
        y_b, (w_gate_bf, w_branch_bf, w_out_bf) = _ssd_mixer(
            uf, z_col, xbc_col, aux, ssd_conv_w[l], ssd_conv_b[l], a_neg_row,
            jnp.repeat(ssd_d[l], SSD_HEAD_DIM), ssd_norm_g[l], bsz, s_len, mix, n_ssd,
            [(w_gate, l), (w_branch, l), (w_out, l)])
        y_c = _hgrn_mixer(uf, hq_col, hf_col, hg_col, ub, hi_col, lb_all[l],
                          jnp.tile(hgrn_norm_g[l], mix // HGRN_HEAD_DIM), bsz, s_len, mix)
        merged, w_up_bf = _gated_merge(ub, gate_col, rank, y_a, y_b, y_c,
                                       w_gate_bf, b_gate[l], w_branch_bf, (w_up, l))
        z1 = _matmul(merged, w_out_bf, F32, residual=xf, alpha=alpha)
        xf, xb = _layernorm(z1, ln1_g[l], ln1_b[l])
        hmid, w_down_bf = _matmul(xb, w_up_bf, BF16, act="relu2", side=(w_down, l))
        z2 = _matmul_ktiled_residual(hmid, w_down_bf, xf, alpha)
        xf, xb = _layernorm(z2, ln2_g[l], ln2_b[l])
    return xf.reshape(bsz, s_len, d).astype(x.dtype)
```

```python
import functools
import math

import numpy as np
import jax
import jax.numpy as jnp
from jax import lax
from jax.experimental import pallas as pl
from jax.experimental.pallas import tpu as pltpu

F32 = jnp.float32
BF16 = jnp.bfloat16

LANE = 128
SUBLANE = 8
MXU_DIM = 256
VMEM_BYTES = 64 * 1024 * 1024
VMEM_CAP = VMEM_BYTES - 6 * 1024 * 1024

FOX_HEAD_DIM = 128
SSD_HEAD_DIM = 64
SSD_STATE = 128
SSD_CHUNK = 128
SSD_GROUP_WIDTH = 256
HGRN_HEAD_DIM = 128
HGRN_BLOCK = 256
HGRN_MM_LEVELS = 3
HGRN_GROUP = 8
LN_EPS = 1e-5
RMS_EPS = 1e-6
DT_COL = 8


def _pick_tile(n, cap, quantum=MXU_DIM):
    for q in (quantum, LANE):
        best = 0
        t = q
        while t <= min(n, cap):
            if n % t == 0:
                best = t
            t += q
        if best:
            return best
    return n


def _params(semantics, vmem_estimate):
    limit = int(min(max(vmem_estimate * 5 // 4 + (2 << 20), 16 << 20), VMEM_CAP))
    return pltpu.CompilerParams(dimension_semantics=semantics, vmem_limit_bytes=limit)


def _sigmoid(x):
    return 1.0 / (1.0 + jnp.exp(-x))


def _split3(x):
    hi = x.astype(BF16)
    r = x - hi.astype(F32)
    mid = r.astype(BF16)
    lo = (r - mid.astype(F32)).astype(BF16)
    return hi, mid, lo


def _dot_sel_lhs(sel, x):
    hi, mid, lo = _split3(x)
    d = lambda p: jnp.dot(sel, p, preferred_element_type=F32)
    return d(hi) + (d(mid) + d(lo))


def _dot_sel_rhs(x, sel):
    hi, mid, lo = _split3(x)
    d = lambda p: jnp.dot(p, sel, preferred_element_type=F32)
    return d(hi) + (d(mid) + d(lo))


def _dot_nt(a, b):
    return lax.dot_general(a, b, (((1,), (1,)), ((), ())), preferred_element_type=F32)


def _dot_tn(a, b):
    return lax.dot_general(a, b, (((0,), (0,)), ((), ())), preferred_element_type=F32)


SIDE_ROWS = 16


def _side_plan(side, grid):
    stack, layer = side
    shape = stack.shape[1:]
    cols = shape[-1]
    total_rows = math.prod(shape[:-1])
    steps = grid[0] * grid[1]
    slabs = steps
    while slabs > 1 and (steps % slabs or total_rows % (slabs * SIDE_ROWS)):
        slabs -= 1
    assert total_rows % (slabs * SIDE_ROWS) == 0
    hold = steps // slabs
    rows = total_rows // slabs
    in_spec = pl.BlockSpec((rows, cols), lambda i, j: (layer * slabs + (i * grid[1] + j) // hold, 0))
    out_spec = pl.BlockSpec((rows, cols), lambda i, j: ((i * grid[1] + j) // hold, 0))
    out_shape = jax.ShapeDtypeStruct((total_rows, cols), BF16)
    return stack.reshape(stack.shape[0] * total_rows, cols), in_spec, out_spec, out_shape, rows * cols, shape


def _mm_kernel(a_ref, w_ref, *refs, act, has_side):
    o_ref = refs[1] if has_side else refs[0]
    acc = jnp.dot(a_ref[...], w_ref[...], preferred_element_type=F32)
    if act == "relu2":
        acc = jnp.square(jnp.maximum(acc, 0.0))
    o_ref[...] = acc.astype(o_ref.dtype)
    if has_side:
        refs[2][...] = refs[0][...].astype(BF16)


def _residual_tile(z_ref, st_ref, g_ref, b_ref):
    st = st_ref[...]
    return (z_ref[...] - st[:, 0:1]) * st[:, 1:2] * g_ref[...] + b_ref[...]


def _mm_res_kernel(a_ref, w_ref, z_ref, st_ref, g_ref, b_ref, o_ref, *, alpha):
    acc = jnp.dot(a_ref[...], w_ref[...], preferred_element_type=F32)
    o_ref[...] = alpha * _residual_tile(z_ref, st_ref, g_ref, b_ref) + acc


def _matmul(a, w, out_dtype, act=None, residual=None, alpha=None, side=None, layer=None, tm_cap=1024, tn_cap=1024):
    m, k = a.shape
    n = w.shape[-1]
    tm = _pick_tile(m, tm_cap)
    tn = _pick_tile(n, tn_cap)
    grid = (m // tm, n // tn)
    osz = jnp.dtype(out_dtype).itemsize
    est = 2 * (tm * k * 2 + k * tn * 2 + tm * tn * osz) + tm * tn * 4
    tile = pl.BlockSpec((tm, tn), lambda i, j: (i, j))
    w_spec = (pl.BlockSpec((k, tn), lambda i, j: (0, j)) if layer is None
              else pl.BlockSpec((None, k, tn), lambda i, j: (layer, 0, j)))
    in_specs = [pl.BlockSpec((tm, k), lambda i, j: (i, 0)), w_spec]
    args = (a, w)
    out_specs = tile
    out_shape = jax.ShapeDtypeStruct((m, n), out_dtype)
    body = functools.partial(_mm_kernel, act=act, has_side=side is not None)
    if residual is not None:
        assert act is None and out_dtype == F32 and side is None
        est += 2 * (tm * tn * 4 + tm * LANE * 4) + 2 * tm * tn * 4
        z_prev, stats, gain, bias = residual
        in_specs += [tile, pl.BlockSpec((tm, LANE), lambda i, j: (i, 0)),
                     pl.BlockSpec((1, tn), lambda i, j: (0, j)), pl.BlockSpec((1, tn), lambda i, j: (0, j))]
        args = (a, w, z_prev, stats, gain.reshape(1, n), bias.reshape(1, n))
        body = functools.partial(_mm_res_kernel, alpha=alpha)
    if side is not None:
        view, s_in, s_out, sshape, selems, side_shape = _side_plan(side, grid)
        est += 2 * selems * (4 + 2)
        in_specs.append(s_in)
        args = (a, w, view)
        out_specs = [tile, s_out]
        out_shape = [out_shape, sshape]
    res = pl.pallas_call(
        body,
        grid=grid,
        in_specs=in_specs,
        out_specs=out_specs,
        out_shape=out_shape,
        compiler_params=_params(("parallel", "parallel"), est),
        name="matmul_" + (act or ("residual" if residual is not None else "plain")),
    )(*args)
    if side is not None:
        return res[0], res[1].reshape(side_shape)
    return res


def _mm_acc_kernel(a_ref, w_ref, z_ref, st_ref, g_ref, b_ref, o_ref, acc_ref, *, alpha):
    kk = pl.program_id(2)

    @pl.when(kk == 0)
    def _():
        acc_ref[...] = jnp.zeros_like(acc_ref)

    acc_ref[...] += jnp.dot(a_ref[...], w_ref[...], preferred_element_type=F32)

    @pl.when(kk == pl.num_programs(2) - 1)
    def _():
        o_ref[...] = alpha * _residual_tile(z_ref, st_ref, g_ref, b_ref) + acc_ref[...]


def _matmul_ktiled_residual(a, w, residual, alpha, tm_cap=1024, tn_cap=2048, tk_cap=1024):
    m, k = a.shape
    n = w.shape[1]
    tm = _pick_tile(m, tm_cap)
    tn = _pick_tile(n, tn_cap)
    tk = _pick_tile(k, tk_cap)
    est = 2 * (tm * tk * 2 + tk * tn * 2 + 2 * tm * tn * 4 + tm * LANE * 4) + 2 * tm * tn * 4
    tile = pl.BlockSpec((tm, tn), lambda i, j, kk: (i, j))
    vec = pl.BlockSpec((1, tn), lambda i, j, kk: (0, j))
    z_prev, stats, gain, bias = residual
    return pl.pallas_call(
        functools.partial(_mm_acc_kernel, alpha=alpha),
        grid=(m // tm, n // tn, k // tk),
        in_specs=[pl.BlockSpec((tm, tk), lambda i, j, kk: (i, kk)),
                  pl.BlockSpec((tk, tn), lambda i, j, kk: (kk, j)),
                  tile, pl.BlockSpec((tm, LANE), lambda i, j, kk: (i, 0)), vec, vec],
        out_specs=tile,
        out_shape=jax.ShapeDtypeStruct((m, n), F32),
        scratch_shapes=[pltpu.VMEM((tm, tn), F32)],
        compiler_params=_params(("parallel", "parallel", "arbitrary"), est),
        name="matmul_ktiled_residual",
    )(a, w, z_prev, stats, gain.reshape(1, n), bias.reshape(1, n))


def _regroup_kernel(w_ref, *out_refs, plans):
    for o_ref, plan in zip(out_refs, plans):
        dst = 0
        for src, width in plan:
            if src is None:
                o_ref[0, :, dst:dst + width] = jnp.zeros((o_ref.shape[1], width), o_ref.dtype)
            else:
                o_ref[0, :, dst:dst + width] = w_ref[0, :, src:src + width].astype(o_ref.dtype)
            dst += width


def _regroup_weights(w, plans):
    depth, k, n = w.shape
    tr = _pick_tile(k, 256, 16)
    widths = [sum(width for _, width in plan) for plan in plans]
    est = 2 * tr * (n * 4 + sum(widths) * 2) + tr * n * 4
    return pl.pallas_call(
        functools.partial(_regroup_kernel, plans=plans),
        grid=(depth, k // tr),
        in_specs=[pl.BlockSpec((1, tr, n), lambda l, i: (l, i, 0))],
        out_specs=[pl.BlockSpec((1, tr, wd), lambda l, i: (l, i, 0)) for wd in widths],
        out_shape=[jax.ShapeDtypeStruct((depth, k, wd), BF16) for wd in widths],
        compiler_params=_params(("parallel", "parallel"), est),
        name="regroup_weights",
    )(w)


def _ln_kernel(z_ref, g_ref, b_ref, o_ref, *st_refs, eps):
    z = z_ref[...]
    mu = jnp.mean(z, axis=-1, keepdims=True)
    zc = z - mu
    var = jnp.mean(zc * zc, axis=-1, keepdims=True)
    rstd = lax.rsqrt(var + eps)
    out = zc * rstd * g_ref[...] + b_ref[...]
    o_ref[...] = out.astype(o_ref.dtype)
    if st_refs:
        lane = lax.broadcasted_iota(jnp.int32, st_refs[0].shape, 1)
        st_refs[0][...] = jnp.where(lane == 0, mu, jnp.where(lane == 1, rstd, 0.0))


def _layernorm(z, g, b, final=False):
    m, d = z.shape
    tm = _pick_tile(m, 256, SUBLANE)
    est = 2 * tm * d * (4 + 4) + 4 * tm * d * 4
    row = pl.BlockSpec((tm, d), lambda i: (i, 0))
    vec = pl.BlockSpec((1, d), lambda i: (0, 0))
    st = pl.BlockSpec((tm, LANE), lambda i: (i, 0))
    return pl.pallas_call(
        functools.partial(_ln_kernel, eps=LN_EPS),
        grid=(m // tm,),
        in_specs=[row, vec, vec],
        out_specs=row if final else [row, st],
        out_shape=(jax.ShapeDtypeStruct((m, d), F32) if final
                   else [jax.ShapeDtypeStruct((m, d), BF16), jax.ShapeDtypeStruct((m, LANE), F32)]),
        compiler_params=_params(("parallel",), est),
        name="layernorm",
    )(z, g.reshape(1, d), b.reshape(1, d))


def _gates_kernel(x_ref, w_ref, bias_ref, tri_ref, aux_ref, ct_ref, carry_ref, *, n_fox):
    blk = pl.program_id(1)

    @pl.when(blk == 0)
    def _():
        carry_ref[...] = jnp.zeros_like(carry_ref)

    s = jnp.dot(x_ref[...], w_ref[...], preferred_element_type=F32) + bias_ref[...]
    soft = jnp.log(1.0 + jnp.exp(-jnp.abs(s)))
    log_sig = jnp.minimum(s, 0.0) - soft
    softplus = jnp.maximum(s, 0.0) + soft
    csum = carry_ref[...] + _dot_sel_lhs(tri_ref[...], log_sig)
    carry_ref[...] = csum[-1:, :]
    lane = lax.broadcasted_iota(jnp.int32, s.shape, 1)
    aux = jnp.where(lane < n_fox, csum, softplus)
    aux_ref[...] = aux
    ct_ref[0] = aux.T[:SUBLANE, :]


def _small_gates(x_bf, w_small, layer, bias_small, bsz, s_len, n_fox):
    m, d = x_bf.shape
    t = _pick_tile(s_len, 512, LANE)
    nblk = s_len // t
    tri = jnp.asarray(np.tril(np.ones((t, t), np.float32)), BF16)
    est = 2 * (t * d * 2 + d * LANE * 2 + t * t * 2 + t * LANE * 4 * 2) + 8 * t * LANE * 4
    return pl.pallas_call(
        functools.partial(_gates_kernel, n_fox=n_fox),
        grid=(bsz, nblk),
        in_specs=[pl.BlockSpec((t, d), lambda b, i: (b * nblk + i, 0)),
                  pl.BlockSpec((None, d, LANE), lambda b, i: (layer, 0, 0)),
                  pl.BlockSpec((1, LANE), lambda b, i: (0, 0)),
                  pl.BlockSpec((t, t), lambda b, i: (0, 0))],
        out_specs=[pl.BlockSpec((t, LANE), lambda b, i: (b * nblk + i, 0)),
                   pl.BlockSpec((1, SUBLANE, t), lambda b, i: (b, 0, i))],
        out_shape=[jax.ShapeDtypeStruct((m, LANE), F32),
                   jax.ShapeDtypeStruct((bsz, SUBLANE, s_len), F32)],
        scratch_shapes=[pltpu.VMEM((1, LANE), F32)],
        compiler_params=_params(("parallel", "arbitrary"), est),
        name="small_gates",
    )(x_bf, w_small, bias_small, tri)


FOX_STRIP = 64
FOX_GROUP = 4
LOG2E = 1.4426950408889634


def _fox_kernel(q_ref, k_ref, v_ref, ct_ref, o_ref, vaug_ref, m_ref, acc_ref, p_ref, *, tq, scale):
    hp = pl.program_id(1)
    i = pl.program_id(2)
    D = FOX_HEAD_DIM
    grp = vaug_ref.shape[0]
    k1 = scale * LOG2E
    nstrip = tq // FOX_STRIP
    s_len = k_ref.shape[0]

    @pl.when(i == 0)
    def _():
        ones_col = (lax.broadcasted_iota(jnp.int32, (s_len, D), 1) == 0).astype(BF16)
        for hh in range(grp):
            vaug_ref[hh, :, 0:D] = v_ref[:, hh * D:(hh + 1) * D]
            vaug_ref[hh, :, D:2 * D] = ones_col

    def block(j, masked):
        ks = pl.multiple_of(j * tq, tq)
        for hh in range(grp):
            hs = slice(hh * D, (hh + 1) * D)
            c2 = ct_ref[0, pl.ds(hp * grp + hh, 1), pl.ds(ks, tq)] * LOG2E
            raw = _dot_nt(q_ref[:, hs], k_ref[pl.ds(ks, tq), hs])
            for r in range(nstrip):
                rs = slice(r * FOX_STRIP, (r + 1) * FOX_STRIP)
                cw = min(tq, -(-((r + 1) * FOX_STRIP) // D) * D) if masked else tq
                s = raw[rs, :cw] * k1 - c2[:, :cw]
                if masked:
                    row = lax.broadcasted_iota(jnp.int32, (FOX_STRIP, cw), 0) + r * FOX_STRIP
                    col = lax.broadcasted_iota(jnp.int32, (FOX_STRIP, cw), 1)
                    s = jnp.where(col <= row, s, -jnp.inf)
                m_new = jnp.broadcast_to(jnp.max(s, axis=1, keepdims=True), (FOX_STRIP, D))
                if not masked:
                    m_old = m_ref[hh, rs, :]
                    m_new = jnp.maximum(m_old, m_new)
                    corr = jnp.exp2(m_old - m_new)
                    acc_ref[hh, rs, :] = acc_ref[hh, rs, :] * jnp.concatenate([corr, corr], axis=1)
                p = jnp.exp2(s - jnp.concatenate([m_new] * (cw // D), axis=1))
                m_ref[hh, rs, :] = m_new
                p_ref[hh, rs, :cw] = p.astype(BF16)
                if cw < tq:
                    p_ref[hh, rs, cw:] = jnp.zeros((FOX_STRIP, tq - cw), BF16)
            pv = jnp.dot(p_ref[hh], vaug_ref[hh, pl.ds(ks, tq), :], preferred_element_type=F32)
            if masked:
                acc_ref[hh] = pv
            else:
                acc_ref[hh] += pv

    block(i, True)

    def body(j, carry):
        block(j, False)
        return carry

    lax.fori_loop(0, i, body, 0)
    for hh in range(grp):
        acc = acc_ref[hh]
        o_ref[:, hh * D:(hh + 1) * D] = (acc[:, 0:D] / acc[:, D:D + 1]).astype(o_ref.dtype)


def _fox_attention(u, ct, bsz, s_len, mix, n_heads):
    m = u.shape[0]
    tq = _pick_tile(s_len, 512, LANE)
    nq = s_len // tq
    grp = math.gcd(n_heads, FOX_GROUP)
    pw = grp * FOX_HEAD_DIM
    assert tq % FOX_STRIP == 0
    hb = mix // pw
    est = (2 * (2 * tq * pw * 2 + 2 * s_len * pw * 2 + SUBLANE * s_len * 4)
           + grp * (2 * s_len * FOX_HEAD_DIM * 2 + tq * LANE * 4 + tq * pw * 4 + tq * tq * 2) + 6 * tq * tq * 4)
    return pl.pallas_call(
        functools.partial(_fox_kernel, tq=tq, scale=FOX_HEAD_DIM ** -0.5),
        grid=(bsz, n_heads // grp, nq),
        in_specs=[pl.BlockSpec((tq, pw), lambda b, h, i: (b * nq + i, h)),
                  pl.BlockSpec((s_len, pw), lambda b, h, i: (b, hb + h)),
                  pl.BlockSpec((s_len, pw), lambda b, h, i: (b, 2 * hb + h)),
                  pl.BlockSpec((1, SUBLANE, s_len), lambda b, h, i: (b, 0, 0))],
        out_specs=pl.BlockSpec((tq, pw), lambda b, h, i: (b * nq + i, h)),
        out_shape=jax.ShapeDtypeStruct((m, mix), BF16),
        scratch_shapes=[pltpu.VMEM((grp, s_len, 2 * FOX_HEAD_DIM), BF16),
                        pltpu.VMEM((grp, tq, FOX_HEAD_DIM), F32),
                        pltpu.VMEM((grp, tq, 2 * FOX_HEAD_DIM), F32),
                        pltpu.VMEM((grp, tq, tq), BF16)],
        compiler_params=_params(("parallel", "parallel", "arbitrary"), est),
        name="fox_attention",
    )(u, u, u, ct)


def _ssd_kernel(z_ref, xbc_ref, aux_ref, cw_ref, cb_ref, aneg_ref, e_ref, tri_ref, d_ref, g_ref,
                *refs, mix, groups, kconv, n_side):
    side_in = refs[:n_side]
    o_ref = refs[n_side]
    side_out = refs[n_side + 1:2 * n_side + 1]
    xext_ref, state_ref = refs[2 * n_side + 1:]
    for s_in, s_out in zip(side_in, side_out):
        s_out[...] = s_in[...].astype(BF16)
    L = SSD_CHUNK
    N = SSD_STATE
    GW = SSD_GROUP_WIDTH
    c = pl.program_id(1)

    @pl.when(c == 0)
    def _():
        state_ref[...] = jnp.zeros_like(state_ref)
        xext_ref[0:SUBLANE, :] = jnp.zeros((SUBLANE, xext_ref.shape[1]), F32)

    xext_ref[SUBLANE:SUBLANE + L, :] = xbc_ref[...]
    conv = cb_ref[...] + cw_ref[0:1, :] * xext_ref[pl.ds(SUBLANE - (kconv - 1), L), :]
    for j in range(1, kconv):
        conv = conv + cw_ref[j:j + 1, :] * xext_ref[pl.ds(SUBLANE - (kconv - 1) + j, L), :]
    xext_ref[0:SUBLANE, :] = xext_ref[L:L + SUBLANE, :]
    xbc = conv * _sigmoid(conv)
    xs = xbc[:, :mix]

    aux = aux_ref[...]
    dta = aux * aneg_ref[...]
    a_all = _dot_sel_lhs(tri_ref[...], dta)
    a_all_t = a_all.T
    e = e_ref[...]
    dt_exp = _dot_sel_rhs(aux, e)
    a_exp = _dot_sel_rhs(a_all, e)
    a_last = a_exp[L - 1:L, :]
    ea = jnp.exp(a_exp)
    xc = xs * dt_exp
    xd = (xc * jnp.exp(a_last - a_exp)).astype(BF16)

    row = lax.broadcasted_iota(jnp.int32, (L, L), 0)
    col = lax.broadcasted_iota(jnp.int32, (L, L), 1)
    causal = col <= row
    lane_lo = lax.broadcasted_iota(jnp.int32, (L, LANE), 1) < SSD_HEAD_DIM

    y_parts = []
    for g in range(groups):
        bg = xbc[:, mix + g * N: mix + (g + 1) * N].astype(BF16)
        cg = xbc[:, mix + (groups + g) * N: mix + (groups + g + 1) * N].astype(BF16)
        cbm = _dot_nt(cg, bg)
        st = state_ref[g]
        gsl = slice(g * GW, (g + 1) * GW)
        y_off = jnp.dot(cg, st.astype(BF16), preferred_element_type=F32) * ea[:, gsl]
        state_ref[g] = ea[L - 1:L, gsl] * st + _dot_tn(bg, xd[:, gsl])
        y_cols = []
        for pair in range(GW // LANE):
            c0 = g * GW + pair * LANE
            xcp = xc[:, c0:c0 + LANE]
            acc = None
            for half in range(LANE // SSD_HEAD_DIM):
                hh = DT_COL + c0 // SSD_HEAD_DIM + half
                seg = a_all[:, hh:hh + 1] - a_all_t[hh:hh + 1, :]
                dec = jnp.where(causal, jnp.exp(jnp.minimum(seg, 0.0)), 0.0)
                gm = (cbm * dec).astype(BF16)
                xm = jnp.where(lane_lo if half == 0 else jnp.logical_not(lane_lo), xcp, 0.0).astype(BF16)
                part = jnp.dot(gm, xm, preferred_element_type=F32)
                acc = part if acc is None else acc + part
            y_cols.append(acc)
        y_diag = jnp.concatenate(y_cols, axis=1)
        y = y_diag + y_off + d_ref[:, gsl] * xs[:, gsl]
        zg = z_ref[:, gsl]
        yz = y * (zg * _sigmoid(zg))
        ms = jnp.mean(yz * yz, axis=1, keepdims=True)
        y_parts.append(yz * lax.rsqrt(ms + RMS_EPS) * g_ref[:, gsl])
    o_ref[...] = jnp.concatenate(y_parts, axis=1).astype(o_ref.dtype)


def _ssd_mixer(u, z_col, xbc_col, aux, conv_w, conv_b, a_neg_row, d_exp, norm_g, bsz, s_len, mix, n_heads, sides):
    m = u.shape[0]
    L = SSD_CHUNK
    groups = mix // SSD_GROUP_WIDTH
    width = mix + 2 * groups * SSD_STATE
    kconv = conv_w.shape[0]
    nchunk = s_len // L
    assert z_col % mix == 0 and xbc_col % width == 0
    z_blk = z_col // mix
    xbc_blk = xbc_col // width
    expand = np.zeros((LANE, mix), np.float32)
    for hh in range(n_heads):
        expand[DT_COL + hh, hh * SSD_HEAD_DIM:(hh + 1) * SSD_HEAD_DIM] = 1.0
    tri = np.tril(np.ones((L, L), np.float32))
    est = 2 * (L * mix * 4 + L * width * 4 + L * LANE * 4 + LANE * mix * 2 + L * mix * 2) + 24 * L * width * 4
    plans = [_side_plan(side, (bsz, nchunk)) for side in sides]
    est += sum(2 * p[4] * 6 for p in plans)
    res = pl.pallas_call(
        functools.partial(_ssd_kernel, mix=mix, groups=groups, kconv=kconv, n_side=len(sides)),
        grid=(bsz, nchunk),
        in_specs=[pl.BlockSpec((L, mix), lambda b, c: (b * nchunk + c, z_blk)),
                  pl.BlockSpec((L, width), lambda b, c: (b * nchunk + c, xbc_blk)),
                  pl.BlockSpec((L, LANE), lambda b, c: (b * nchunk + c, 0)),
                  pl.BlockSpec((kconv, width), lambda b, c: (0, 0)),
                  pl.BlockSpec((1, width), lambda b, c: (0, 0)),
                  pl.BlockSpec((1, LANE), lambda b, c: (0, 0)),
                  pl.BlockSpec((LANE, mix), lambda b, c: (0, 0)),
                  pl.BlockSpec((L, L), lambda b, c: (0, 0)),
                  pl.BlockSpec((1, mix), lambda b, c: (0, 0)),
                  pl.BlockSpec((1, mix), lambda b, c: (0, 0))] + [p[1] for p in plans],
        out_specs=[pl.BlockSpec((L, mix), lambda b, c: (b * nchunk + c, 0))] + [p[2] for p in plans],
        out_shape=[jax.ShapeDtypeStruct((m, mix), BF16)] + [p[3] for p in plans],
        scratch_shapes=[pltpu.VMEM((L + SUBLANE, width), F32),
                        pltpu.VMEM((groups, SSD_STATE, SSD_GROUP_WIDTH), F32)],
        compiler_params=_params(("parallel", "arbitrary"), est),
        name="ssd_mixer",
    )(u, u, aux, conv_w, conv_b.reshape(1, width), a_neg_row, jnp.asarray(expand, BF16),
      jnp.asarray(tri, BF16), d_exp.reshape(1, mix), norm_g.reshape(1, mix), *[p[0] for p in plans])
    return res[0], [r.reshape(p[5]) for r, p in zip(res[1:], plans)]


def _hgrn_tables(blk):
    nlev = int(np.log2(blk))
    tri = np.tril(np.ones((blk, blk), np.float32))
    rows = []
    t = np.arange(blk)
    for l in range(1, HGRN_MM_LEVELS + 1):
        size = 1 << l
        mid = (t // size) * size + size // 2 - 1
        sign = np.where(t > mid, 1.0, -1.0).astype(np.float32)[:, None]
        rows.append(sign * (tri - tri[mid]))
    wlev = np.concatenate(rows, axis=0)
    th = np.arange(blk // 2)
    x = th[:, None] ^ th[None, :]
    lv = np.where(x > 0, np.floor(np.log2(np.maximum(x, 1))).astype(np.int32) + 1, 0)
    lv = np.where(th[None, :] > th[:, None], -1, lv).astype(np.int32)
    return tri, wlev, lv, nlev


def _hgrn_kernel(q_ref, f_ref, i_ref, g_ref, lb_ref, ng_ref, tri_ref, wlev_ref, lv_ref, o_ref, state_ref,
                 *, blk, nlev, n_heads, grp):
    c = pl.program_id(1)

    @pl.when(c == 0)
    def _():
        state_ref[...] = jnp.zeros_like(state_ref)

    half = blk // 2
    top = slice(0, half)
    bot = slice(half, blk)
    pw = grp * HGRN_HEAD_DIM

    def head_pair(p, carry):
        sl = pl.ds(pl.multiple_of(p * pw, pw), pw)
        hq = q_ref[:, sl]
        lb = lb_ref[:, sl]
        f = lb + (1.0 - lb) * _sigmoid(f_ref[:, sl])
        g2 = jnp.log(f) * LOG2E
        k = 1.0 - f
        q = hq * _sigmoid(hq)
        hi, mid, lo = _split3(g2)
        dot = lambda w, x: jnp.dot(w, x, preferred_element_type=F32)
        tri = tri_ref[...]
        b = dot(tri, hi) + (dot(tri, mid) + dot(tri, lo))
        wlev = wlev_ref[...]
        dlev = dot(wlev, hi) + dot(wlev, mid)
        facs = []
        for l in range(1, nlev + 1):
            if l <= HGRN_MM_LEVELS:
                dneg = dlev[(l - 1) * blk:l * blk]
            else:
                size = 1 << l
                hs2 = size // 2
                pieces = []
                for r0 in range(0, blk, size):
                    bmid = b[r0 + hs2 - 1:r0 + hs2]
                    pieces.append(bmid - b[r0:r0 + hs2])
                    pieces.append(b[r0 + hs2:r0 + size] - bmid)
                dneg = jnp.concatenate(pieces, axis=0)
            facs.append(jnp.exp2(dneg).astype(BF16))
        eb = jnp.exp2(b)
        b_last = b[blk - 1:blk, :]
        e_last = jnp.exp2(b_last)
        kdec = (k * jnp.exp2(b_last - b)).astype(BF16)
        qe = (q * eb).astype(BF16)
        qb2 = q.astype(BF16)
        kb2 = k.astype(BF16)
        hg = g_ref[:, sl]
        gate = ng_ref[:, sl] * (hg * _sigmoid(hg))
        lv = lv_ref[...]
        for hh in range(grp):
            hs = slice(hh * HGRN_HEAD_DIM, (hh + 1) * HGRN_HEAD_DIM)
            hidx = p * grp + hh
            osl = pl.ds(pl.multiple_of(hidx * HGRN_HEAD_DIM, HGRN_HEAD_DIM), HGRN_HEAD_DIM)
            qb = qb2[:, hs]
            kb = kb2[:, hs]
            vb = i_ref[:, osl]
            a_tl = jnp.where(lv == 0, _dot_nt(qb[top], kb[top]), 0.0)
            a_br = jnp.where(lv == 0, _dot_nt(qb[bot], kb[bot]), 0.0)
            for l in range(1, nlev):
                fl = facs[l - 1][:, hs]
                qf = qb * fl
                kf = kb * fl
                a_tl = jnp.where(lv == l, _dot_nt(qf[top], kf[top]), a_tl)
                a_br = jnp.where(lv == l, _dot_nt(qf[bot], kf[bot]), a_br)
            ft = facs[nlev - 1][:, hs]
            a_bl = _dot_nt(qb[bot] * ft[bot], kb[top] * ft[top])
            st = state_ref[hidx]
            o_top = jnp.dot(a_tl.astype(BF16), vb[top], preferred_element_type=F32)
            o_bot = (jnp.dot(a_bl.astype(BF16), vb[top], preferred_element_type=F32)
                     + jnp.dot(a_br.astype(BF16), vb[bot], preferred_element_type=F32))
            o = jnp.concatenate([o_top, o_bot], axis=0)
            o = o + _dot_nt(qe[:, hs], st.astype(BF16))
            state_ref[hidx] = e_last[:, hs] * st + _dot_tn(vb, kdec[:, hs])
            ms = jnp.mean(o * o, axis=1, keepdims=True)
            y = o * lax.rsqrt(ms + RMS_EPS) * gate[:, hs]
            o_ref[:, osl] = y.astype(o_ref.dtype)
        return carry

    lax.fori_loop(0, n_heads // grp, head_pair, 0)


def _hgrn_mixer(uf, q_col, f_col, g_col, ub, i_col, lb, norm_g, bsz, s_len, mix):
    m = uf.shape[0]
    blk = _pick_tile(s_len, HGRN_BLOCK, LANE)
    n_heads = mix // HGRN_HEAD_DIM
    tri, wlev, lv, nlev = _hgrn_tables(blk)
    assert nlev > HGRN_MM_LEVELS
    grp = math.gcd(n_heads, HGRN_GROUP)
    nblk = s_len // blk
    spec = lambda col: pl.BlockSpec((blk, mix), lambda b, c, k=col // mix: (b * nblk + c, k))
    vec = pl.BlockSpec((1, mix), lambda b, c: (0, 0))
    pw = grp * HGRN_HEAD_DIM
    est = (2 * (3 * blk * mix * 4 + 2 * blk * mix * 2 + (tri.size + wlev.size) * 2 + lv.size * 4)
           + (2 * wlev.shape[0] + (nlev + 16) * blk) * pw * 4 + 12 * blk * blk * 4)
    return pl.pallas_call(
        functools.partial(_hgrn_kernel, blk=blk, nlev=nlev, n_heads=n_heads, grp=grp),
        grid=(bsz, nblk),
        in_specs=[spec(q_col), spec(f_col), spec(i_col), spec(g_col), vec, vec,
                  pl.BlockSpec(tri.shape, lambda b, c: (0, 0)),
                  pl.BlockSpec(wlev.shape, lambda b, c: (0, 0)),
                  pl.BlockSpec(lv.shape, lambda b, c: (0, 0))],
        out_specs=pl.BlockSpec((blk, mix), lambda b, c: (b * nblk + c, 0)),
        out_shape=jax.ShapeDtypeStruct((m, mix), BF16),
        scratch_shapes=[pltpu.VMEM((n_heads, HGRN_HEAD_DIM, HGRN_HEAD_DIM), F32)],
        compiler_params=_params(("parallel", "arbitrary"), est),
        name="hgrn2_mixer",
    )(uf, uf, ub, uf, lb.reshape(1, mix), norm_g.reshape(1, mix), jnp.asarray(tri, BF16),
      jnp.asarray(wlev, BF16), jnp.asarray(lv))


def _merge_kernel(gl_ref, ya_ref, yb_ref, yc_ref, wg_ref, bg_ref, wb_ref, side_ref, o_ref, side_out_ref):
    gl = gl_ref[...]
    acc = None
    for i, y_ref in enumerate((ya_ref, yb_ref, yc_ref)):
        gate = _sigmoid(jnp.dot(gl, wg_ref[i], preferred_element_type=F32) + bg_ref[i])
        contrib = gate * jnp.dot(y_ref[...], wb_ref[i], preferred_element_type=F32)
        acc = contrib if acc is None else acc + contrib
    o_ref[...] = acc.astype(o_ref.dtype)
    side_out_ref[...] = side_ref[...].astype(BF16)


def _gated_merge(u, gate_col, rank, ya, yb, yc, w_gate, b_gate, w_branch, side):
    m = u.shape[0]
    mix = ya.shape[1]
    nbr, _, d = w_gate.shape
    tm = _pick_tile(m, 1024)
    tn = _pick_tile(d, 512)
    grid = (m // tm, d // tn)
    ybs = pl.BlockSpec((tm, mix), lambda i, j: (i, 0))
    view, s_in, s_out, sshape, selems, side_shape = _side_plan(side, grid)
    est = (2 * (tm * rank * 2 + 3 * tm * mix * 2 + nbr * rank * tn * 2 + nbr * mix * tn * 2 + tm * tn * 2 + selems * 6)
           + 6 * tm * tn * 4)
    tile = pl.BlockSpec((tm, tn), lambda i, j: (i, j))
    merged, side_bf = pl.pallas_call(
        _merge_kernel,
        grid=grid,
        in_specs=[pl.BlockSpec((tm, rank), lambda i, j: (i, gate_col // rank)),
                  ybs, ybs, ybs,
                  pl.BlockSpec((nbr, rank, tn), lambda i, j: (0, 0, j)),
                  pl.BlockSpec((nbr, 1, tn), lambda i, j: (0, 0, j)),
                  pl.BlockSpec((nbr, mix, tn), lambda i, j: (0, 0, j)),
                  s_in],
        out_specs=[tile, s_out],
        out_shape=[jax.ShapeDtypeStruct((m, d), BF16), sshape],
        compiler_params=_params(("parallel", "parallel"), est),
        name="gated_merge",
    )(u, ya, yb, yc, w_gate, b_gate.reshape(nbr, 1, d), w_branch, view)
    return merged, side_bf.reshape(side_shape)


def kernel(x, w_in, fox_f_bias, ssd_conv_w, ssd_conv_b, ssd_dt_bias, ssd_a_log, ssd_d, ssd_norm_g, hgrn_lower_bound, hgrn_norm_g, w_gate, b_gate, w_branch, w_out, ln1_g, ln1_b, w_up, w_down, ln2_g, ln2_b):
    bsz, s_len, d = x.shape
    depth = w_in.shape[0]
    mix = d // 4
    n_fox = fox_f_bias.shape[1]
    n_ssd = ssd_dt_bias.shape[1]
    rank = w_gate.shape[2]
    conv_dim = ssd_conv_w.shape[2]
    alpha = (2 * depth) ** 0.25
    assert mix % SSD_GROUP_WIDTH == 0 and conv_dim == 2 * mix and n_fox <= DT_COL and DT_COL + n_ssd <= LANE
    assert n_fox * FOX_HEAD_DIM == mix and n_ssd * SSD_HEAD_DIM == mix
    assert s_len % SSD_CHUNK == 0 and (4 * mix) % rank == 0

    widths = (mix, mix, mix, n_fox, mix, conv_dim, n_ssd, mix, mix, mix, mix, rank)
    offs = np.concatenate([[0], np.cumsum(widths)])
    col = lambda k: slice(int(offs[k]), int(offs[k + 1]))
    bf_groups = (0, 1, 2, 9, 11)
    f32_groups = (5, 4, 7, 8, 10)
    hi_col, gate_col = 3 * mix, 4 * mix
    xbc_col, z_col, hq_col, hf_col, hg_col = 0, conv_dim, conv_dim + mix, conv_dim + 2 * mix, conv_dim + 3 * mix

    lb_all = jnp.cumsum(jax.nn.softmax(hgrn_lower_bound.astype(F32), axis=0), axis=0)
    lb_all = lb_all - lb_all[0]

    m = bsz * s_len
    xf = x.reshape(m, d).astype(F32)
    xb = xf.astype(BF16)
    identity_stats = jnp.zeros((m, LANE), F32).at[:, 1].set(1.0)
    res = (xf, identity_stats, jnp.ones((d,), F32), jnp.zeros((d,), F32))
    pad = LANE - DT_COL - n_ssd
    piece = lambda k: (int(offs[k]), int(widths[k]))
    small_plan = [piece(3)] + ([(None, DT_COL - n_fox)] if n_fox < DT_COL else []) + [piece(6)] + ([(None, pad)] if pad else [])
    w_bf_all, w_f32_all, w_small_all = _regroup_weights(
        w_in.astype(F32), [[piece(k) for k in bf_groups], [piece(k) for k in f32_groups], small_plan])
    for l in range(depth):
        bias_small = jnp.concatenate(
            [fox_f_bias[l], jnp.zeros((DT_COL - n_fox,), F32), ssd_dt_bias[l], jnp.zeros((pad,), F32)]).reshape(1, LANE)
        a_neg_row = jnp.concatenate(
            [jnp.zeros((DT_COL,), F32), -jnp.exp(ssd_a_log[l].astype(F32)), jnp.zeros((pad,), F32)]).reshape(1, LANE)

        ub = _matmul(xb, w_bf_all, BF16, layer=l, tn_cap=768)
        uf = _matmul(xb, w_f32_all, F32, layer=l)
        aux, ct = _small_gates(xb, w_small_all, l, bias_small, bsz, s_len, n_fox)
        y_a = _fox_attention(ub, ct, bsz, s_len, mix, n_fox)
        y_b, (w_gate_bf, w_branch_bf, w_out_bf) = _ssd_mixer(
            uf, z_col, xbc_col, aux, ssd_conv_w[l], ssd_conv_b[l], a_neg_row,
            jnp.repeat(ssd_d[l], SSD_HEAD_DIM), ssd_norm_g[l], bsz, s_len, mix, n_ssd,
            [(w_gate, l), (w_branch, l), (w_out, l)])
        y_c = _hgrn_mixer(uf, hq_col, hf_col, hg_col, ub, hi_col, lb_all[l],
                          jnp.tile(hgrn_norm_g[l], mix // HGRN_HEAD_DIM), bsz, s_len, mix)
        merged, w_up_bf = _gated_merge(ub, gate_col, rank, y_a, y_b, y_c,
                                       w_gate_bf, b_gate[l], w_branch_bf, (w_up, l))
        z1 = _matmul(merged, w_out_bf, F32, residual=res, alpha=alpha)
        xb, st1 = _layernorm(z1, ln1_g[l], ln1_b[l])
        res = (z1, st1, ln1_g[l].astype(F32), ln1_b[l].astype(F32))
        hmid, w_down_bf = _matmul(xb, w_up_bf, BF16, act="relu2", side=(w_down, l))
        z2 = _matmul_ktiled_residual(hmid, w_down_bf, res, alpha)
        if l == depth - 1:
            out = _layernorm(z2, ln2_g[l], ln2_b[l], final=True)
        else:
            xb, st2 = _layernorm(z2, ln2_g[l], ln2_b[l])
            res = (z2, st2, ln2_g[l].astype(F32), ln2_b[l].astype(F32))
    return out.reshape(bsz, s_len, d).astype(x.dtype)
```

```python
import functools
import math

import numpy as np
import jax
import jax.numpy as jnp
from jax import lax
from jax.experimental import pallas as pl
from jax.experimental.pallas import tpu as pltpu

F32 = jnp.float32
BF16 = jnp.bfloat16

LANE = 128
SUBLANE = 8
MXU_DIM = 256
VMEM_BYTES = 64 * 1024 * 1024
VMEM_CAP = VMEM_BYTES - 6 * 1024 * 1024

FOX_HEAD_DIM = 128
SSD_HEAD_DIM = 64
SSD_STATE = 128
SSD_CHUNK = 128
SSD_GROUP_WIDTH = 256
HGRN_HEAD_DIM = 128
HGRN_BLOCK = 256
HGRN_MM_LEVELS = 3
HGRN_GROUP = 8
LN_EPS = 1e-5
RMS_EPS = 1e-6
DT_COL = 8


def _pick_tile(n, cap, quantum=MXU_DIM):
    for q in (quantum, LANE):
        best = 0
        t = q
        while t <= min(n, cap):
            if n % t == 0:
                best = t
            t += q
        if best:
            return best
    return n


def _params(semantics, vmem_estimate):
    limit = int(min(max(vmem_estimate * 5 // 4 + (2 << 20), 16 << 20), VMEM_CAP))
    return pltpu.CompilerParams(dimension_semantics=semantics, vmem_limit_bytes=limit)


def _sigmoid(x):
    return 1.0 / (1.0 + jnp.exp(-x))


def _split3(x):
    hi = x.astype(BF16)
    r = x - hi.astype(F32)
    mid = r.astype(BF16)
    lo = (r - mid.astype(F32)).astype(BF16)
    return hi, mid, lo


def _dot_sel_lhs(sel, x):
    hi, mid, lo = _split3(x)
    d = lambda p: jnp.dot(sel, p, preferred_element_type=F32)
    return d(hi) + (d(mid) + d(lo))


def _dot_sel_rhs(x, sel):
    hi, mid, lo = _split3(x)
    d = lambda p: jnp.dot(p, sel, preferred_element_type=F32)
    return d(hi) + (d(mid) + d(lo))


def _dot_nt(a, b):
    return lax.dot_general(a, b, (((1,), (1,)), ((), ())), preferred_element_type=F32)


def _dot_tn(a, b):
    return lax.dot_general(a, b, (((0,), (0,)), ((), ())), preferred_element_type=F32)


SIDE_ROWS = 16


def _side_plan(side, grid):
    stack, layer = side
    shape = stack.shape[1:]
    cols = shape[-1]
    total_rows = math.prod(shape[:-1])
    steps = grid[0] * grid[1]
    slabs = steps
    while slabs > 1 and (steps % slabs or total_rows % (slabs * SIDE_ROWS)):
        slabs -= 1
    assert total_rows % (slabs * SIDE_ROWS) == 0
    hold = steps // slabs
    rows = total_rows // slabs
    in_spec = pl.BlockSpec((rows, cols), lambda i, j: (layer * slabs + (i * grid[1] + j) // hold, 0))
    out_spec = pl.BlockSpec((rows, cols), lambda i, j: ((i * grid[1] + j) // hold, 0))
    out_shape = jax.ShapeDtypeStruct((total_rows, cols), BF16)
    return stack.reshape(stack.shape[0] * total_rows, cols), in_spec, out_spec, out_shape, rows * cols, shape


def _mm_kernel(a_ref, w_ref, *refs, act, has_side):
    o_ref = refs[1] if has_side else refs[0]
    acc = jnp.dot(a_ref[...], w_ref[...], preferred_element_type=F32)
    if act == "relu2":
        acc = jnp.square(jnp.maximum(acc, 0.0))
    o_ref[...] = acc.astype(o_ref.dtype)
    if has_side:
        refs[2][...] = refs[0][...].astype(BF16)


def _residual_tile(z_ref, st_ref, g_ref, b_ref, alpha):
    st = st_ref[...]
    return (z_ref[...] - st[:, 0:1]) * (alpha * st[:, 1:2]) * g_ref[...] + alpha * b_ref[...]


def _mm_res_kernel(a_ref, w_ref, z_ref, st_ref, g_ref, b_ref, o_ref, *, alpha):
    acc = jnp.dot(a_ref[...], w_ref[...], preferred_element_type=F32)
    o_ref[...] = _residual_tile(z_ref, st_ref, g_ref, b_ref, alpha) + acc


def _matmul(a, w, out_dtype, act=None, residual=None, alpha=None, side=None, layer=None, tm_cap=1024, tn_cap=1024):
    m, k = a.shape
    n = w.shape[-1]
    tm = _pick_tile(m, tm_cap)
    tn = _pick_tile(n, tn_cap)
    grid = (m // tm, n // tn)
    osz = jnp.dtype(out_dtype).itemsize
    est = 2 * (tm * k * 2 + k * tn * 2 + tm * tn * osz) + tm * tn * 4
    tile = pl.BlockSpec((tm, tn), lambda i, j: (i, j))
    w_spec = (pl.BlockSpec((k, tn), lambda i, j: (0, j)) if layer is None
              else pl.BlockSpec((None, k, tn), lambda i, j: (layer, 0, j)))
    in_specs = [pl.BlockSpec((tm, k), lambda i, j: (i, 0)), w_spec]
    args = (a, w)
    out_specs = tile
    out_shape = jax.ShapeDtypeStruct((m, n), out_dtype)
    body = functools.partial(_mm_kernel, act=act, has_side=side is not None)
    if residual is not None:
        assert act is None and out_dtype == F32 and side is None
        est += 2 * (tm * tn * 4 + tm * LANE * 4) + 2 * tm * tn * 4
        z_prev, stats, gain, bias = residual
        in_specs += [tile, pl.BlockSpec((tm, LANE), lambda i, j: (i, 0)),
                     pl.BlockSpec((1, tn), lambda i, j: (0, j)), pl.BlockSpec((1, tn), lambda i, j: (0, j))]
        args = (a, w, z_prev, stats, gain.reshape(1, n), bias.reshape(1, n))
        body = functools.partial(_mm_res_kernel, alpha=alpha)
    if side is not None:
        view, s_in, s_out, sshape, selems, side_shape = _side_plan(side, grid)
        est += 2 * selems * (4 + 2)
        in_specs.append(s_in)
        args = (a, w, view)
        out_specs = [tile, s_out]
        out_shape = [out_shape, sshape]
    res = pl.pallas_call(
        body,
        grid=grid,
        in_specs=in_specs,
        out_specs=out_specs,
        out_shape=out_shape,
        compiler_params=_params(("parallel", "parallel"), est),
        name="matmul_" + (act or ("residual" if residual is not None else "plain")),
    )(*args)
    if side is not None:
        return res[0], res[1].reshape(side_shape)
    return res


def _mm_acc_kernel(a_ref, w_ref, z_ref, st_ref, g_ref, b_ref, o_ref, *, alpha):
    @pl.when(pl.program_id(2) == 0)
    def _():
        o_ref[...] = _residual_tile(z_ref, st_ref, g_ref, b_ref, alpha)

    o_ref[...] += jnp.dot(a_ref[...], w_ref[...], preferred_element_type=F32)


def _matmul_ktiled_residual(a, w, residual, alpha, tm_cap=1024, tn_cap=2048, tk_cap=1024):
    m, k = a.shape
    n = w.shape[1]
    tm = _pick_tile(m, tm_cap)
    tn = _pick_tile(n, tn_cap)
    tk = _pick_tile(k, tk_cap)
    est = 2 * (tm * tk * 2 + tk * tn * 2 + 2 * tm * tn * 4 + tm * LANE * 4) + 2 * tm * tn * 4
    tile = pl.BlockSpec((tm, tn), lambda i, j, kk: (i, j))
    vec = pl.BlockSpec((1, tn), lambda i, j, kk: (0, j))
    z_prev, stats, gain, bias = residual
    return pl.pallas_call(
        functools.partial(_mm_acc_kernel, alpha=alpha),
        grid=(m // tm, n // tn, k // tk),
        in_specs=[pl.BlockSpec((tm, tk), lambda i, j, kk: (i, kk)),
                  pl.BlockSpec((tk, tn), lambda i, j, kk: (kk, j)),
                  tile, pl.BlockSpec((tm, LANE), lambda i, j, kk: (i, 0)), vec, vec],
        out_specs=tile,
        out_shape=jax.ShapeDtypeStruct((m, n), F32),
        compiler_params=_params(("parallel", "parallel", "arbitrary"), est),
        name="matmul_ktiled_residual",
    )(a, w, z_prev, stats, gain.reshape(1, n), bias.reshape(1, n))


def _regroup_kernel(w_ref, *out_refs, plans):
    for o_ref, plan in zip(out_refs, plans):
        dst = 0
        for src, width in plan:
            if src is None:
                o_ref[0, :, dst:dst + width] = jnp.zeros((o_ref.shape[1], width), o_ref.dtype)
            else:
                o_ref[0, :, dst:dst + width] = w_ref[0, :, src:src + width].astype(o_ref.dtype)
            dst += width


def _regroup_weights(w, plans):
    depth, k, n = w.shape
    tr = _pick_tile(k, 256, 16)
    widths = [sum(width for _, width in plan) for plan in plans]
    est = 2 * tr * (n * 4 + sum(widths) * 2) + tr * n * 4
    return pl.pallas_call(
        functools.partial(_regroup_kernel, plans=plans),
        grid=(depth, k // tr),
        in_specs=[pl.BlockSpec((1, tr, n), lambda l, i: (l, i, 0))],
        out_specs=[pl.BlockSpec((1, tr, wd), lambda l, i: (l, i, 0)) for wd in widths],
        out_shape=[jax.ShapeDtypeStruct((depth, k, wd), BF16) for wd in widths],
        compiler_params=_params(("parallel", "parallel"), est),
        name="regroup_weights",
    )(w)


def _ln_kernel(z_ref, g_ref, b_ref, o_ref, *st_refs, eps):
    z = z_ref[...]
    mu = jnp.mean(z, axis=-1, keepdims=True)
    zc = z - mu
    var = jnp.mean(zc * zc, axis=-1, keepdims=True)
    rstd = lax.rsqrt(var + eps)
    out = zc * rstd * g_ref[...] + b_ref[...]
    o_ref[...] = out.astype(o_ref.dtype)
    if st_refs:
        lane = lax.broadcasted_iota(jnp.int32, st_refs[0].shape, 1)
        st_refs[0][...] = jnp.where(lane == 0, mu, jnp.where(lane == 1, rstd, 0.0))


def _layernorm(z, g, b, final=False):
    m, d = z.shape
    tm = _pick_tile(m, 256, SUBLANE)
    est = 2 * tm * d * (4 + 4) + 4 * tm * d * 4
    row = pl.BlockSpec((tm, d), lambda i: (i, 0))
    vec = pl.BlockSpec((1, d), lambda i: (0, 0))
    st = pl.BlockSpec((tm, LANE), lambda i: (i, 0))
    return pl.pallas_call(
        functools.partial(_ln_kernel, eps=LN_EPS),
        grid=(m // tm,),
        in_specs=[row, vec, vec],
        out_specs=row if final else [row, st],
        out_shape=(jax.ShapeDtypeStruct((m, d), F32) if final
                   else [jax.ShapeDtypeStruct((m, d), BF16), jax.ShapeDtypeStruct((m, LANE), F32)]),
        compiler_params=_params(("parallel",), est),
        name="layernorm",
    )(z, g.reshape(1, d), b.reshape(1, d))


def _gates_kernel(x_ref, w_ref, bias_ref, tri_ref, aux_ref, ct_ref, carry_ref, *, n_fox):
    blk = pl.program_id(1)

    @pl.when(blk == 0)
    def _():
        carry_ref[...] = jnp.zeros_like(carry_ref)

    s = jnp.dot(x_ref[...], w_ref[...], preferred_element_type=F32) + bias_ref[...]
    soft = jnp.log(1.0 + jnp.exp(-jnp.abs(s)))
    log_sig = jnp.minimum(s, 0.0) - soft
    softplus = jnp.maximum(s, 0.0) + soft
    csum = carry_ref[...] + _dot_sel_lhs(tri_ref[...], log_sig)
    carry_ref[...] = csum[-1:, :]
    lane = lax.broadcasted_iota(jnp.int32, s.shape, 1)
    aux = jnp.where(lane < n_fox, csum, softplus)
    aux_ref[...] = aux
    ct_ref[0] = aux.T[:SUBLANE, :]


def _small_gates(x_bf, w_small, layer, bias_small, bsz, s_len, n_fox):
    m, d = x_bf.shape
    t = _pick_tile(s_len, 512, LANE)
    nblk = s_len // t
    tri = jnp.asarray(np.tril(np.ones((t, t), np.float32)), BF16)
    est = 2 * (t * d * 2 + d * LANE * 2 + t * t * 2 + t * LANE * 4 * 2) + 8 * t * LANE * 4
    return pl.pallas_call(
        functools.partial(_gates_kernel, n_fox=n_fox),
        grid=(bsz, nblk),
        in_specs=[pl.BlockSpec((t, d), lambda b, i: (b * nblk + i, 0)),
                  pl.BlockSpec((None, d, LANE), lambda b, i: (layer, 0, 0)),
                  pl.BlockSpec((1, LANE), lambda b, i: (0, 0)),
                  pl.BlockSpec((t, t), lambda b, i: (0, 0))],
        out_specs=[pl.BlockSpec((t, LANE), lambda b, i: (b * nblk + i, 0)),
                   pl.BlockSpec((1, SUBLANE, t), lambda b, i: (b, 0, i))],
        out_shape=[jax.ShapeDtypeStruct((m, LANE), F32),
                   jax.ShapeDtypeStruct((bsz, SUBLANE, s_len), F32)],
        scratch_shapes=[pltpu.VMEM((1, LANE), F32)],
        compiler_params=_params(("parallel", "arbitrary"), est),
        name="small_gates",
    )(x_bf, w_small, bias_small, tri)


FOX_STRIP = 64
FOX_GROUP = 4
LOG2E = 1.4426950408889634


def _fox_kernel(q_ref, k_ref, v_ref, ct_ref, o_ref, vaug_ref, m_ref, acc_ref, p_ref, *, tq, scale):
    hp = pl.program_id(1)
    i = pl.program_id(2)
    D = FOX_HEAD_DIM
    grp = vaug_ref.shape[0]
    k1 = scale * LOG2E
    nstrip = tq // FOX_STRIP
    s_len = k_ref.shape[0]

    @pl.when(i == 0)
    def _():
        ones_col = (lax.broadcasted_iota(jnp.int32, (s_len, D), 1) == 0).astype(BF16)
        for hh in range(grp):
            vaug_ref[hh, :, 0:D] = v_ref[:, hh * D:(hh + 1) * D]
            vaug_ref[hh, :, D:2 * D] = ones_col

    def block(j, masked):
        ks = pl.multiple_of(j * tq, tq)
        for hh in range(grp):
            hs = slice(hh * D, (hh + 1) * D)
            c2 = ct_ref[0, pl.ds(hp * grp + hh, 1), pl.ds(ks, tq)] * LOG2E
            raw = _dot_nt(q_ref[:, hs], k_ref[pl.ds(ks, tq), hs])
            for r in range(nstrip):
                rs = slice(r * FOX_STRIP, (r + 1) * FOX_STRIP)
                cw = min(tq, -(-((r + 1) * FOX_STRIP) // D) * D) if masked else tq
                s = raw[rs, :cw] * k1 - c2[:, :cw]
                if masked:
                    row = lax.broadcasted_iota(jnp.int32, (FOX_STRIP, cw), 0) + r * FOX_STRIP
                    col = lax.broadcasted_iota(jnp.int32, (FOX_STRIP, cw), 1)
                    s = jnp.where(col <= row, s, -jnp.inf)
                m_new = jnp.broadcast_to(jnp.max(s, axis=1, keepdims=True), (FOX_STRIP, D))
                if not masked:
                    m_old = m_ref[hh, rs, :]
                    m_new = jnp.maximum(m_old, m_new)
                    corr = jnp.exp2(m_old - m_new)
                    acc_ref[hh, rs, :] = acc_ref[hh, rs, :] * jnp.concatenate([corr, corr], axis=1)
                p = jnp.exp2(s - jnp.concatenate([m_new] * (cw // D), axis=1))
                m_ref[hh, rs, :] = m_new
                p_ref[hh, rs, :cw] = p.astype(BF16)
                if cw < tq:
                    p_ref[hh, rs, cw:] = jnp.zeros((FOX_STRIP, tq - cw), BF16)
            pv = jnp.dot(p_ref[hh], vaug_ref[hh, pl.ds(ks, tq), :], preferred_element_type=F32)
            if masked:
                acc_ref[hh] = pv
            else:
                acc_ref[hh] += pv

    block(i, True)

    def body(j, carry):
        block(j, False)
        return carry

    lax.fori_loop(0, i, body, 0)
    for hh in range(grp):
        acc = acc_ref[hh]
        o_ref[:, hh * D:(hh + 1) * D] = (acc[:, 0:D] / acc[:, D:D + 1]).astype(o_ref.dtype)


def _fox_attention(u, ct, bsz, s_len, mix, n_heads):
    m = u.shape[0]
    tq = _pick_tile(s_len, 512, LANE)
    nq = s_len // tq
    grp = math.gcd(n_heads, FOX_GROUP)
    pw = grp * FOX_HEAD_DIM
    assert tq % FOX_STRIP == 0
    hb = mix // pw
    est = (2 * (2 * tq * pw * 2 + 2 * s_len * pw * 2 + SUBLANE * s_len * 4)
           + grp * (2 * s_len * FOX_HEAD_DIM * 2 + tq * LANE * 4 + tq * pw * 4 + tq * tq * 2) + 6 * tq * tq * 4)
    return pl.pallas_call(
        functools.partial(_fox_kernel, tq=tq, scale=FOX_HEAD_DIM ** -0.5),
        grid=(bsz, n_heads // grp, nq),
        in_specs=[pl.BlockSpec((tq, pw), lambda b, h, i: (b * nq + i, h)),
                  pl.BlockSpec((s_len, pw), lambda b, h, i: (b, hb + h)),
                  pl.BlockSpec((s_len, pw), lambda b, h, i: (b, 2 * hb + h)),
                  pl.BlockSpec((1, SUBLANE, s_len), lambda b, h, i: (b, 0, 0))],
        out_specs=pl.BlockSpec((tq, pw), lambda b, h, i: (b * nq + i, h)),
        out_shape=jax.ShapeDtypeStruct((m, mix), BF16),
        scratch_shapes=[pltpu.VMEM((grp, s_len, 2 * FOX_HEAD_DIM), BF16),
                        pltpu.VMEM((grp, tq, FOX_HEAD_DIM), F32),
                        pltpu.VMEM((grp, tq, 2 * FOX_HEAD_DIM), F32),
                        pltpu.VMEM((grp, tq, tq), BF16)],
        compiler_params=_params(("parallel", "parallel", "arbitrary"), est),
        name="fox_attention",
    )(u, u, u, ct)


def _ssd_kernel(z_ref, xbc_ref, aux_ref, cw_ref, cb_ref, aneg_ref, e_ref, tri_ref, d_ref, g_ref,
                *refs, mix, groups, kconv, n_side):
    side_in = refs[:n_side]
    o_ref = refs[n_side]
    side_out = refs[n_side + 1:2 * n_side + 1]
    xext_ref, state_ref = refs[2 * n_side + 1:]
    for s_in, s_out in zip(side_in, side_out):
        s_out[...] = s_in[...].astype(BF16)
    L = SSD_CHUNK
    N = SSD_STATE
    GW = SSD_GROUP_WIDTH
    c = pl.program_id(1)

    @pl.when(c == 0)
    def _():
        state_ref[...] = jnp.zeros_like(state_ref)
        xext_ref[0:SUBLANE, :] = jnp.zeros((SUBLANE, xext_ref.shape[1]), F32)

    xext_ref[SUBLANE:SUBLANE + L, :] = xbc_ref[...]
    conv = cb_ref[...] + cw_ref[0:1, :] * xext_ref[pl.ds(SUBLANE - (kconv - 1), L), :]
    for j in range(1, kconv):
        conv = conv + cw_ref[j:j + 1, :] * xext_ref[pl.ds(SUBLANE - (kconv - 1) + j, L), :]
    xext_ref[0:SUBLANE, :] = xext_ref[L:L + SUBLANE, :]
    xbc = conv * _sigmoid(conv)
    xs = xbc[:, :mix]

    aux = aux_ref[...]
    dta = aux * aneg_ref[...]
    a_all = _dot_sel_lhs(tri_ref[...], dta)
    a_all_t = a_all.T
    e = e_ref[...]
    dt_exp = _dot_sel_rhs(aux, e)
    a_exp = _dot_sel_rhs(a_all, e)
    a_last = a_exp[L - 1:L, :]
    ea = jnp.exp(a_exp)
    xc = xs * dt_exp
    xd = (xc * jnp.exp(a_last - a_exp)).astype(BF16)

    row = lax.broadcasted_iota(jnp.int32, (L, L), 0)
    col = lax.broadcasted_iota(jnp.int32, (L, L), 1)
    causal = col <= row
    lane_lo = lax.broadcasted_iota(jnp.int32, (L, LANE), 1) < SSD_HEAD_DIM

    y_parts = []
    for g in range(groups):
        bg = xbc[:, mix + g * N: mix + (g + 1) * N].astype(BF16)
        cg = xbc[:, mix + (groups + g) * N: mix + (groups + g + 1) * N].astype(BF16)
        cbm = _dot_nt(cg, bg)
        st = state_ref[g]
        gsl = slice(g * GW, (g + 1) * GW)
        y_off = jnp.dot(cg, st.astype(BF16), preferred_element_type=F32) * ea[:, gsl]
        state_ref[g] = ea[L - 1:L, gsl] * st + _dot_tn(bg, xd[:, gsl])
        y_cols = []
        for pair in range(GW // LANE):
            c0 = g * GW + pair * LANE
            xcp = xc[:, c0:c0 + LANE]
            acc = None
            for half in range(LANE // SSD_HEAD_DIM):
                hh = DT_COL + c0 // SSD_HEAD_DIM + half
                seg = a_all[:, hh:hh + 1] - a_all_t[hh:hh + 1, :]
                dec = jnp.where(causal, jnp.exp(jnp.minimum(seg, 0.0)), 0.0)
                gm = (cbm * dec).astype(BF16)
                xm = jnp.where(lane_lo if half == 0 else jnp.logical_not(lane_lo), xcp, 0.0).astype(BF16)
                part = jnp.dot(gm, xm, preferred_element_type=F32)
                acc = part if acc is None else acc + part
            y_cols.append(acc)
        y_diag = jnp.concatenate(y_cols, axis=1)
        y = y_diag + y_off + d_ref[:, gsl] * xs[:, gsl]
        zg = z_ref[:, gsl]
        yz = y * (zg * _sigmoid(zg))
        ms = jnp.mean(yz * yz, axis=1, keepdims=True)
        y_parts.append(yz * lax.rsqrt(ms + RMS_EPS) * g_ref[:, gsl])
    o_ref[...] = jnp.concatenate(y_parts, axis=1).astype(o_ref.dtype)


def _ssd_mixer(u, z_col, xbc_col, aux, conv_w, conv_b, a_neg_row, d_exp, norm_g, bsz, s_len, mix, n_heads, sides):
    m = u.shape[0]
    L = SSD_CHUNK
    groups = mix // SSD_GROUP_WIDTH
    width = mix + 2 * groups * SSD_STATE
    kconv = conv_w.shape[0]
    nchunk = s_len // L
    assert z_col % mix == 0 and xbc_col % width == 0
    z_blk = z_col // mix
    xbc_blk = xbc_col // width
    expand = np.zeros((LANE, mix), np.float32)
    for hh in range(n_heads):
        expand[DT_COL + hh, hh * SSD_HEAD_DIM:(hh + 1) * SSD_HEAD_DIM] = 1.0
    tri = np.tril(np.ones((L, L), np.float32))
    est = 2 * (L * mix * 4 + L * width * 4 + L * LANE * 4 + LANE * mix * 2 + L * mix * 2) + 24 * L * width * 4
    plans = [_side_plan(side, (bsz, nchunk)) for side in sides]
    est += sum(2 * p[4] * 6 for p in plans)
    res = pl.pallas_call(
        functools.partial(_ssd_kernel, mix=mix, groups=groups, kconv=kconv, n_side=len(sides)),
        grid=(bsz, nchunk),
        in_specs=[pl.BlockSpec((L, mix), lambda b, c: (b * nchunk + c, z_blk)),
                  pl.BlockSpec((L, width), lambda b, c: (b * nchunk + c, xbc_blk)),
                  pl.BlockSpec((L, LANE), lambda b, c: (b * nchunk + c, 0)),
                  pl.BlockSpec((kconv, width), lambda b, c: (0, 0)),
                  pl.BlockSpec((1, width), lambda b, c: (0, 0)),
                  pl.BlockSpec((1, LANE), lambda b, c: (0, 0)),
                  pl.BlockSpec((LANE, mix), lambda b, c: (0, 0)),
                  pl.BlockSpec((L, L), lambda b, c: (0, 0)),
                  pl.BlockSpec((1, mix), lambda b, c: (0, 0)),
                  pl.BlockSpec((1, mix), lambda b, c: (0, 0))] + [p[1] for p in plans],
        out_specs=[pl.BlockSpec((L, mix), lambda b, c: (b * nchunk + c, 0))] + [p[2] for p in plans],
        out_shape=[jax.ShapeDtypeStruct((m, mix), BF16)] + [p[3] for p in plans],
        scratch_shapes=[pltpu.VMEM((L + SUBLANE, width), F32),
                        pltpu.VMEM((groups, SSD_STATE, SSD_GROUP_WIDTH), F32)],
        compiler_params=_params(("parallel", "arbitrary"), est),
        name="ssd_mixer",
    )(u, u, aux, conv_w, conv_b.reshape(1, width), a_neg_row, jnp.asarray(expand, BF16),
      jnp.asarray(tri, BF16), d_exp.reshape(1, mix), norm_g.reshape(1, mix), *[p[0] for p in plans])
    return res[0], [r.reshape(p[5]) for r, p in zip(res[1:], plans)]


def _hgrn_tables(blk):
    nlev = int(np.log2(blk))
    tri = np.tril(np.ones((blk, blk), np.float32))
    rows = []
    t = np.arange(blk)
    for l in range(1, HGRN_MM_LEVELS + 1):
        size = 1 << l
        mid = (t // size) * size + size // 2 - 1
        sign = np.where(t > mid, 1.0, -1.0).astype(np.float32)[:, None]
        rows.append(sign * (tri - tri[mid]))
    wlev = np.concatenate(rows, axis=0)
    th = np.arange(blk // 2)
    x = th[:, None] ^ th[None, :]
    lv = np.where(x > 0, np.floor(np.log2(np.maximum(x, 1))).astype(np.int32) + 1, 0)
    lv = np.where(th[None, :] > th[:, None], -1, lv).astype(np.int32)
    return tri, wlev, lv, nlev


def _hgrn_kernel(q_ref, f_ref, i_ref, g_ref, lb_ref, ng_ref, tri_ref, wlev_ref, lv_ref, o_ref, state_ref,
                 *, blk, nlev, n_heads, grp):
    c = pl.program_id(1)

    @pl.when(c == 0)
    def _():
        state_ref[...] = jnp.zeros_like(state_ref)

    half = blk // 2
    top = slice(0, half)
    bot = slice(half, blk)
    pw = grp * HGRN_HEAD_DIM

    def head_pair(p, carry):
        sl = pl.ds(pl.multiple_of(p * pw, pw), pw)
        hq = q_ref[:, sl]
        lb = lb_ref[:, sl]
        f = lb + (1.0 - lb) * _sigmoid(f_ref[:, sl])
        g2 = jnp.log(f) * LOG2E
        k = 1.0 - f
        q = hq * _sigmoid(hq)
        hi, mid, lo = _split3(g2)
        dot = lambda w, x: jnp.dot(w, x, preferred_element_type=F32)
        tri = tri_ref[...]
        b = dot(tri, hi) + (dot(tri, mid) + dot(tri, lo))
        wlev = wlev_ref[...]
        dlev = dot(wlev, hi) + dot(wlev, mid)
        facs = []
        for l in range(1, nlev + 1):
            if l <= HGRN_MM_LEVELS:
                dneg = dlev[(l - 1) * blk:l * blk]
            else:
                size = 1 << l
                hs2 = size // 2
                pieces = []
                for r0 in range(0, blk, size):
                    bmid = b[r0 + hs2 - 1:r0 + hs2]
                    pieces.append(bmid - b[r0:r0 + hs2])
                    pieces.append(b[r0 + hs2:r0 + size] - bmid)
                dneg = jnp.concatenate(pieces, axis=0)
            facs.append(jnp.exp2(dneg).astype(BF16))
        eb = jnp.exp2(b)
        b_last = b[blk - 1:blk, :]
        e_last = jnp.exp2(b_last)
        kdec = (k * jnp.exp2(b_last - b)).astype(BF16)
        qe = (q * eb).astype(BF16)
        qb2 = q.astype(BF16)
        kb2 = k.astype(BF16)
        hg = g_ref[:, sl]
        gate = ng_ref[:, sl] * (hg * _sigmoid(hg))
        lv = lv_ref[...]
        for hh in range(grp):
            hs = slice(hh * HGRN_HEAD_DIM, (hh + 1) * HGRN_HEAD_DIM)
            hidx = p * grp + hh
            osl = pl.ds(pl.multiple_of(hidx * HGRN_HEAD_DIM, HGRN_HEAD_DIM), HGRN_HEAD_DIM)
            qb = qb2[:, hs]
            kb = kb2[:, hs]
            vb = i_ref[:, osl]
            a_tl = jnp.where(lv == 0, _dot_nt(qb[top], kb[top]), 0.0)
            a_br = jnp.where(lv == 0, _dot_nt(qb[bot], kb[bot]), 0.0)
            for l in range(1, nlev):
                fl = facs[l - 1][:, hs]
                qf = qb * fl
                kf = kb * fl
                a_tl = jnp.where(lv == l, _dot_nt(qf[top], kf[top]), a_tl)
                a_br = jnp.where(lv == l, _dot_nt(qf[bot], kf[bot]), a_br)
            ft = facs[nlev - 1][:, hs]
            a_bl = _dot_nt(qb[bot] * ft[bot], kb[top] * ft[top])
            st = state_ref[hidx]
            o_top = jnp.dot(a_tl.astype(BF16), vb[top], preferred_element_type=F32)
            o_bot = (jnp.dot(a_bl.astype(BF16), vb[top], preferred_element_type=F32)
                     + jnp.dot(a_br.astype(BF16), vb[bot], preferred_element_type=F32))
            o = jnp.concatenate([o_top, o_bot], axis=0)
            o = o + _dot_nt(qe[:, hs], st.astype(BF16))
            state_ref[hidx] = e_last[:, hs] * st + _dot_tn(vb, kdec[:, hs])
            ms = jnp.mean(o * o, axis=1, keepdims=True)
            y = o * lax.rsqrt(ms + RMS_EPS) * gate[:, hs]
            o_ref[:, osl] = y.astype(o_ref.dtype)
        return carry

    lax.fori_loop(0, n_heads // grp, head_pair, 0)


def _hgrn_mixer(uf, q_col, f_col, g_col, ub, i_col, lb, norm_g, bsz, s_len, mix):
    m = uf.shape[0]
    blk = _pick_tile(s_len, HGRN_BLOCK, LANE)
    n_heads = mix // HGRN_HEAD_DIM
    tri, wlev, lv, nlev = _hgrn_tables(blk)
    assert nlev > HGRN_MM_LEVELS
    grp = math.gcd(n_heads, HGRN_GROUP)
    nblk = s_len // blk
    spec = lambda col: pl.BlockSpec((blk, mix), lambda b, c, k=col // mix: (b * nblk + c, k))
    vec = pl.BlockSpec((1, mix), lambda b, c: (0, 0))
    pw = grp * HGRN_HEAD_DIM
    est = (2 * (3 * blk * mix * 4 + 2 * blk * mix * 2 + (tri.size + wlev.size) * 2 + lv.size * 4)
           + (2 * wlev.shape[0] + (nlev + 16) * blk) * pw * 4 + 12 * blk * blk * 4)
    return pl.pallas_call(
        functools.partial(_hgrn_kernel, blk=blk, nlev=nlev, n_heads=n_heads, grp=grp),
        grid=(bsz, nblk),
        in_specs=[spec(q_col), spec(f_col), spec(i_col), spec(g_col), vec, vec,
                  pl.BlockSpec(tri.shape, lambda b, c: (0, 0)),
                  pl.BlockSpec(wlev.shape, lambda b, c: (0, 0)),
                  pl.BlockSpec(lv.shape, lambda b, c: (0, 0))],
        out_specs=pl.BlockSpec((blk, mix), lambda b, c: (b * nblk + c, 0)),
        out_shape=jax.ShapeDtypeStruct((m, mix), BF16),
        scratch_shapes=[pltpu.VMEM((n_heads, HGRN_HEAD_DIM, HGRN_HEAD_DIM), F32)],
        compiler_params=_params(("parallel", "arbitrary"), est),
        name="hgrn2_mixer",
    )(uf, uf, ub, uf, lb.reshape(1, mix), norm_g.reshape(1, mix), jnp.asarray(tri, BF16),
      jnp.asarray(wlev, BF16), jnp.asarray(lv))


def _merge_kernel(gl_ref, ya_ref, yb_ref, yc_ref, wg_ref, bg_ref, wb_ref, side_ref, o_ref, side_out_ref):
    gl = gl_ref[...]
    acc = None
    for i, y_ref in enumerate((ya_ref, yb_ref, yc_ref)):
        gate = _sigmoid(jnp.dot(gl, wg_ref[i], preferred_element_type=F32) + bg_ref[i])
        contrib = gate * jnp.dot(y_ref[...], wb_ref[i], preferred_element_type=F32)
        acc = contrib if acc is None else acc + contrib
    o_ref[...] = acc.astype(o_ref.dtype)
    side_out_ref[...] = side_ref[...].astype(BF16)


def _gated_merge(u, gate_col, rank, ya, yb, yc, w_gate, b_gate, w_branch, side):
    m = u.shape[0]
    mix = ya.shape[1]
    nbr, _, d = w_gate.shape
    tm = _pick_tile(m, 1024)
    tn = _pick_tile(d, 512)
    grid = (m // tm, d // tn)
    ybs = pl.BlockSpec((tm, mix), lambda i, j: (i, 0))
    view, s_in, s_out, sshape, selems, side_shape = _side_plan(side, grid)
    est = (2 * (tm * rank * 2 + 3 * tm * mix * 2 + nbr * rank * tn * 2 + nbr * mix * tn * 2 + tm * tn * 2 + selems * 6)
           + 6 * tm * tn * 4)
    tile = pl.BlockSpec((tm, tn), lambda i, j: (i, j))
    merged, side_bf = pl.pallas_call(
        _merge_kernel,
        grid=grid,
        in_specs=[pl.BlockSpec((tm, rank), lambda i, j: (i, gate_col // rank)),
                  ybs, ybs, ybs,
                  pl.BlockSpec((nbr, rank, tn), lambda i, j: (0, 0, j)),
                  pl.BlockSpec((nbr, 1, tn), lambda i, j: (0, 0, j)),
                  pl.BlockSpec((nbr, mix, tn), lambda i, j: (0, 0, j)),
                  s_in],
        out_specs=[tile, s_out],
        out_shape=[jax.ShapeDtypeStruct((m, d), BF16), sshape],
        compiler_params=_params(("parallel", "parallel"), est),
        name="gated_merge",
    )(u, ya, yb, yc, w_gate, b_gate.reshape(nbr, 1, d), w_branch, view)
    return merged, side_bf.reshape(side_shape)


def kernel(x, w_in, fox_f_bias, ssd_conv_w, ssd_conv_b, ssd_dt_bias, ssd_a_log, ssd_d, ssd_norm_g, hgrn_lower_bound, hgrn_norm_g, w_gate, b_gate, w_branch, w_out, ln1_g, ln1_b, w_up, w_down, ln2_g, ln2_b):
    bsz, s_len, d = x.shape
    depth = w_in.shape[0]
    mix = d // 4
    n_fox = fox_f_bias.shape[1]
    n_ssd = ssd_dt_bias.shape[1]
    rank = w_gate.shape[2]
    conv_dim = ssd_conv_w.shape[2]
    alpha = (2 * depth) ** 0.25
    assert mix % SSD_GROUP_WIDTH == 0 and conv_dim == 2 * mix and n_fox <= DT_COL and DT_COL + n_ssd <= LANE
    assert n_fox * FOX_HEAD_DIM == mix and n_ssd * SSD_HEAD_DIM == mix
    assert s_len % SSD_CHUNK == 0 and (4 * mix) % rank == 0

    widths = (mix, mix, mix, n_fox, mix, conv_dim, n_ssd, mix, mix, mix, mix, rank)
    offs = np.concatenate([[0], np.cumsum(widths)])
    col = lambda k: slice(int(offs[k]), int(offs[k + 1]))
    bf_groups = (0, 1, 2, 9, 11)
    f32_groups = (5, 4, 7, 8, 10)
    hi_col, gate_col = 3 * mix, 4 * mix
    xbc_col, z_col, hq_col, hf_col, hg_col = 0, conv_dim, conv_dim + mix, conv_dim + 2 * mix, conv_dim + 3 * mix

    lb_all = jnp.cumsum(jax.nn.softmax(hgrn_lower_bound.astype(F32), axis=0), axis=0)
    lb_all = lb_all - lb_all[0]

    m = bsz * s_len
    xf = x.reshape(m, d).astype(F32)
    xb = xf.astype(BF16)
    identity_stats = jnp.zeros((m, LANE), F32).at[:, 1].set(1.0)
    res = (xf, identity_stats, jnp.ones((d,), F32), jnp.zeros((d,), F32))
    pad = LANE - DT_COL - n_ssd
    piece = lambda k: (int(offs[k]), int(widths[k]))
    small_plan = [piece(3)] + ([(None, DT_COL - n_fox)] if n_fox < DT_COL else []) + [piece(6)] + ([(None, pad)] if pad else [])
    w_bf_all, w_f32_all, w_small_all = _regroup_weights(
        w_in.astype(F32), [[piece(k) for k in bf_groups], [piece(k) for k in f32_groups], small_plan])
    for l in range(depth):
        bias_small = jnp.concatenate(
            [fox_f_bias[l], jnp.zeros((DT_COL - n_fox,), F32), ssd_dt_bias[l], jnp.zeros((pad,), F32)]).reshape(1, LANE)
        a_neg_row = jnp.concatenate(
            [jnp.zeros((DT_COL,), F32), -jnp.exp(ssd_a_log[l].astype(F32)), jnp.zeros((pad,), F32)]).reshape(1, LANE)

        ub = _matmul(xb, w_bf_all, BF16, layer=l, tn_cap=768)
        uf = _matmul(xb, w_f32_all, F32, layer=l)
        aux, ct = _small_gates(xb, w_small_all, l, bias_small, bsz, s_len, n_fox)
        y_a = _fox_attention(ub, ct, bsz, s_len, mix, n_fox)
        y_b, (w_gate_bf, w_branch_bf, w_out_bf) = _ssd_mixer(
            uf, z_col, xbc_col, aux, ssd_conv_w[l], ssd_conv_b[l], a_neg_row,
            jnp.repeat(ssd_d[l], SSD_HEAD_DIM), ssd_norm_g[l], bsz, s_len, mix, n_ssd,
            [(w_gate, l), (w_branch, l), (w_out, l)])
        y_c = _hgrn_mixer(uf, hq_col, hf_col, hg_col, ub, hi_col, lb_all[l],
                          jnp.tile(hgrn_norm_g[l], mix // HGRN_HEAD_DIM), bsz, s_len, mix)
        merged, w_up_bf = _gated_merge(ub, gate_col, rank, y_a, y_b, y_c,
                                       w_gate_bf, b_gate[l], w_branch_bf, (w_up, l))
        z1 = _matmul(merged, w_out_bf, F32, residual=res, alpha=alpha)
        xb, st1 = _layernorm(z1, ln1_g[l], ln1_b[l])
        res = (z1, st1, ln1_g[l].astype(F32), ln1_b[l].astype(F32))
        hmid, w_down_bf = _matmul(xb, w_up_bf, BF16, act="relu2", side=(w_down, l))
        z2 = _matmul_ktiled_residual(hmid, w_down_bf, res, alpha)
        if l == depth - 1:
            out = _layernorm(z2, ln2_g[l], ln2_b[l], final=True)
        else:
            xb, st2 = _layernorm(z2, ln2_g[l], ln2_b[l])
            res = (z2, st2, ln2_g[l].astype(F32), ln2_b[l].astype(F32))
    return out.reshape(bsz, s_len, d).astype(x.dtype)
```

```python
import functools
import math

import numpy as np
import jax
import jax.numpy as jnp
from jax import lax
from jax.experimental import pallas as pl
from jax.experimental.pallas import tpu as pltpu

F32 = jnp.float32
BF16 = jnp.bfloat16

LANE = 128
SUBLANE = 8
MXU_DIM = 256
VMEM_BYTES = 64 * 1024 * 1024
VMEM_CAP = VMEM_BYTES - 6 * 1024 * 1024

FOX_HEAD_DIM = 128
SSD_HEAD_DIM = 64
SSD_STATE = 128
SSD_CHUNK = 128
SSD_GROUP_WIDTH = 256
HGRN_HEAD_DIM = 128
HGRN_BLOCK = 256
HGRN_MM_LEVELS = 3
HGRN_GROUP = 8
LN_EPS = 1e-5
RMS_EPS = 1e-6
DT_COL = 8


def _pick_tile(n, cap, quantum=MXU_DIM):
    for q in (quantum, LANE):
        best = 0
        t = q
        while t <= min(n, cap):
            if n % t == 0:
                best = t
            t += q
        if best:
            return best
    return n


def _params(semantics, vmem_estimate):
    limit = int(min(max(vmem_estimate * 5 // 4 + (2 << 20), 16 << 20), VMEM_CAP))
    return pltpu.CompilerParams(dimension_semantics=semantics, vmem_limit_bytes=limit)


def _sigmoid(x):
    return 1.0 / (1.0 + jnp.exp(-x))


def _split3(x):
    hi = x.astype(BF16)
    r = x - hi.astype(F32)
    mid = r.astype(BF16)
    lo = (r - mid.astype(F32)).astype(BF16)
    return hi, mid, lo


def _dot_sel_lhs(sel, x):
    hi, mid, lo = _split3(x)
    d = lambda p: jnp.dot(sel, p, preferred_element_type=F32)
    return d(hi) + (d(mid) + d(lo))


def _dot_sel_rhs(x, sel):
    hi, mid, lo = _split3(x)
    d = lambda p: jnp.dot(p, sel, preferred_element_type=F32)
    return d(hi) + (d(mid) + d(lo))


def _dot_nt(a, b):
    return lax.dot_general(a, b, (((1,), (1,)), ((), ())), preferred_element_type=F32)


def _dot_tn(a, b):
    return lax.dot_general(a, b, (((0,), (0,)), ((), ())), preferred_element_type=F32)


SIDE_ROWS = 16


def _side_plan(side, grid):
    stack, layer = side
    shape = stack.shape[1:]
    cols = shape[-1]
    total_rows = math.prod(shape[:-1])
    steps = grid[0] * grid[1]
    slabs = steps
    while slabs > 1 and (steps % slabs or total_rows % (slabs * SIDE_ROWS)):
        slabs -= 1
    assert total_rows % (slabs * SIDE_ROWS) == 0
    hold = steps // slabs
    rows = total_rows // slabs
    in_spec = pl.BlockSpec((rows, cols), lambda i, j: (layer * slabs + (i * grid[1] + j) // hold, 0))
    out_spec = pl.BlockSpec((rows, cols), lambda i, j: ((i * grid[1] + j) // hold, 0))
    out_shape = jax.ShapeDtypeStruct((total_rows, cols), BF16)
    return stack.reshape(stack.shape[0] * total_rows, cols), in_spec, out_spec, out_shape, rows * cols, shape


def _mm_kernel(a_ref, w_ref, *refs, act, has_side):
    o_ref = refs[1] if has_side else refs[0]
    acc = jnp.dot(a_ref[...], w_ref[...], preferred_element_type=F32)
    if act == "relu2":
        acc = jnp.square(jnp.maximum(acc, 0.0))
    o_ref[...] = acc.astype(o_ref.dtype)
    if has_side:
        refs[2][...] = refs[0][...].astype(BF16)


def _residual_tile(z_ref, st_ref, g_ref, b_ref, alpha):
    st = st_ref[...]
    return (z_ref[...] - st[:, 0:1]) * (alpha * st[:, 1:2]) * g_ref[...] + alpha * b_ref[...]


def _mm_res_kernel(a_ref, w_ref, z_ref, st_ref, g_ref, b_ref, o_ref, *, alpha):
    acc = jnp.dot(a_ref[...], w_ref[...], preferred_element_type=F32)
    o_ref[...] = _residual_tile(z_ref, st_ref, g_ref, b_ref, alpha) + acc


def _matmul(a, w, out_dtype, act=None, residual=None, alpha=None, side=None, layer=None, tm_cap=1024, tn_cap=1024):
    m, k = a.shape
    n = w.shape[-1]
    tm = _pick_tile(m, tm_cap)
    tn = _pick_tile(n, tn_cap)
    grid = (m // tm, n // tn)
    osz = jnp.dtype(out_dtype).itemsize
    est = 2 * (tm * k * 2 + k * tn * 2 + tm * tn * osz) + tm * tn * 4
    tile = pl.BlockSpec((tm, tn), lambda i, j: (i, j))
    w_spec = (pl.BlockSpec((k, tn), lambda i, j: (0, j)) if layer is None
              else pl.BlockSpec((None, k, tn), lambda i, j: (layer, 0, j)))
    in_specs = [pl.BlockSpec((tm, k), lambda i, j: (i, 0)), w_spec]
    args = (a, w)
    out_specs = tile
    out_shape = jax.ShapeDtypeStruct((m, n), out_dtype)
    body = functools.partial(_mm_kernel, act=act, has_side=side is not None)
    if residual is not None:
        assert act is None and out_dtype == F32 and side is None
        est += 2 * (tm * tn * 4 + tm * LANE * 4) + 2 * tm * tn * 4
        z_prev, stats, gain, bias = residual
        in_specs += [tile, pl.BlockSpec((tm, LANE), lambda i, j: (i, 0)),
                     pl.BlockSpec((1, tn), lambda i, j: (0, j)), pl.BlockSpec((1, tn), lambda i, j: (0, j))]
        args = (a, w, z_prev, stats, gain.reshape(1, n), bias.reshape(1, n))
        body = functools.partial(_mm_res_kernel, alpha=alpha)
    if side is not None:
        view, s_in, s_out, sshape, selems, side_shape = _side_plan(side, grid)
        est += 2 * selems * (4 + 2)
        in_specs.append(s_in)
        args = (a, w, view)
        out_specs = [tile, s_out]
        out_shape = [out_shape, sshape]
    res = pl.pallas_call(
        body,
        grid=grid,
        in_specs=in_specs,
        out_specs=out_specs,
        out_shape=out_shape,
        compiler_params=_params(("parallel", "parallel"), est),
        name="matmul_" + (act or ("residual" if residual is not None else "plain")),
    )(*args)
    if side is not None:
        return res[0], res[1].reshape(side_shape)
    return res


def _mm_acc_kernel(a_ref, w_ref, z_ref, st_ref, g_ref, b_ref, o_ref, *, alpha):
    @pl.when(pl.program_id(2) == 0)
    def _():
        o_ref[...] = _residual_tile(z_ref, st_ref, g_ref, b_ref, alpha)

    o_ref[...] += jnp.dot(a_ref[...], w_ref[...], preferred_element_type=F32)


def _matmul_ktiled_residual(a, w, residual, alpha, tm_cap=1024, tn_cap=2048, tk_cap=1024):
    m, k = a.shape
    n = w.shape[1]
    tm = _pick_tile(m, tm_cap)
    tn = _pick_tile(n, tn_cap)
    tk = _pick_tile(k, tk_cap)
    est = 2 * (tm * tk * 2 + tk * tn * 2 + 2 * tm * tn * 4 + tm * LANE * 4) + 2 * tm * tn * 4
    tile = pl.BlockSpec((tm, tn), lambda i, j, kk: (i, j))
    vec = pl.BlockSpec((1, tn), lambda i, j, kk: (0, j))
    z_prev, stats, gain, bias = residual
    return pl.pallas_call(
        functools.partial(_mm_acc_kernel, alpha=alpha),
        grid=(m // tm, n // tn, k // tk),
        in_specs=[pl.BlockSpec((tm, tk), lambda i, j, kk: (i, kk)),
                  pl.BlockSpec((tk, tn), lambda i, j, kk: (kk, j)),
                  tile, pl.BlockSpec((tm, LANE), lambda i, j, kk: (i, 0)), vec, vec],
        out_specs=tile,
        out_shape=jax.ShapeDtypeStruct((m, n), F32),
        compiler_params=_params(("parallel", "parallel", "arbitrary"), est),
        name="matmul_ktiled_residual",
    )(a, w, z_prev, stats, gain.reshape(1, n), bias.reshape(1, n))


def _regroup_kernel(w_ref, *out_refs, plans):
    for o_ref, plan in zip(out_refs, plans):
        dst = 0
        for src, width in plan:
            if src is None:
                o_ref[0, :, dst:dst + width] = jnp.zeros((o_ref.shape[1], width), o_ref.dtype)
            else:
                o_ref[0, :, dst:dst + width] = w_ref[0, :, src:src + width].astype(o_ref.dtype)
            dst += width


def _regroup_weights(w, plans):
    depth, k, n = w.shape
    tr = _pick_tile(k, 256, 16)
    widths = [sum(width for _, width in plan) for plan in plans]
    est = 2 * tr * (n * 4 + sum(widths) * 2) + tr * n * 4
    return pl.pallas_call(
        functools.partial(_regroup_kernel, plans=plans),
        grid=(depth, k // tr),
        in_specs=[pl.BlockSpec((1, tr, n), lambda l, i: (l, i, 0))],
        out_specs=[pl.BlockSpec((1, tr, wd), lambda l, i: (l, i, 0)) for wd in widths],
        out_shape=[jax.ShapeDtypeStruct((depth, k, wd), BF16) for wd in widths],
        compiler_params=_params(("parallel", "parallel"), est),
        name="regroup_weights",
    )(w)


def _ln_kernel(z_ref, g_ref, b_ref, o_ref, *st_refs, eps):
    z = z_ref[...]
    mu = jnp.mean(z, axis=-1, keepdims=True)
    zc = z - mu
    var = jnp.mean(zc * zc, axis=-1, keepdims=True)
    rstd = lax.rsqrt(var + eps)
    out = zc * rstd * g_ref[...] + b_ref[...]
    o_ref[...] = out.astype(o_ref.dtype)
    if st_refs:
        lane = lax.broadcasted_iota(jnp.int32, st_refs[0].shape, 1)
        st_refs[0][...] = jnp.where(lane == 0, mu, jnp.where(lane == 1, rstd, 0.0))


def _layernorm(z, g, b, final=False):
    m, d = z.shape
    tm = _pick_tile(m, 256, SUBLANE)
    est = 2 * tm * d * (4 + 4) + 4 * tm * d * 4
    row = pl.BlockSpec((tm, d), lambda i: (i, 0))
    vec = pl.BlockSpec((1, d), lambda i: (0, 0))
    st = pl.BlockSpec((tm, LANE), lambda i: (i, 0))
    return pl.pallas_call(
        functools.partial(_ln_kernel, eps=LN_EPS),
        grid=(m // tm,),
        in_specs=[row, vec, vec],
        out_specs=row if final else [row, st],
        out_shape=(jax.ShapeDtypeStruct((m, d), F32) if final
                   else [jax.ShapeDtypeStruct((m, d), BF16), jax.ShapeDtypeStruct((m, LANE), F32)]),
        compiler_params=_params(("parallel",), est),
        name="layernorm",
    )(z, g.reshape(1, d), b.reshape(1, d))


def _gates_kernel(x_ref, w_ref, bias_ref, tri_ref, aux_ref, ct_ref, carry_ref, *, n_fox):
    blk = pl.program_id(1)

    @pl.when(blk == 0)
    def _():
        carry_ref[...] = jnp.zeros_like(carry_ref)

    s = jnp.dot(x_ref[...], w_ref[...], preferred_element_type=F32) + bias_ref[...]
    soft = jnp.log(1.0 + jnp.exp(-jnp.abs(s)))
    log_sig = jnp.minimum(s, 0.0) - soft
    softplus = jnp.maximum(s, 0.0) + soft
    csum = carry_ref[...] + _dot_sel_lhs(tri_ref[...], log_sig)
    carry_ref[...] = csum[-1:, :]
    lane = lax.broadcasted_iota(jnp.int32, s.shape, 1)
    aux = jnp.where(lane < n_fox, csum, softplus)
    aux_ref[...] = aux
    ct_ref[0] = aux.T[:SUBLANE, :]


def _small_gates(x_bf, w_small, layer, bias_small, bsz, s_len, n_fox):
    m, d = x_bf.shape
    t = _pick_tile(s_len, 512, LANE)
    nblk = s_len // t
    tri = jnp.asarray(np.tril(np.ones((t, t), np.float32)), BF16)
    est = 2 * (t * d * 2 + d * LANE * 2 + t * t * 2 + t * LANE * 4 * 2) + 8 * t * LANE * 4
    return pl.pallas_call(
        functools.partial(_gates_kernel, n_fox=n_fox),
        grid=(bsz, nblk),
        in_specs=[pl.BlockSpec((t, d), lambda b, i: (b * nblk + i, 0)),
                  pl.BlockSpec((None, d, LANE), lambda b, i: (layer, 0, 0)),
                  pl.BlockSpec((1, LANE), lambda b, i: (0, 0)),
                  pl.BlockSpec((t, t), lambda b, i: (0, 0))],
        out_specs=[pl.BlockSpec((t, LANE), lambda b, i: (b * nblk + i, 0)),
                   pl.BlockSpec((1, SUBLANE, t), lambda b, i: (b, 0, i))],
        out_shape=[jax.ShapeDtypeStruct((m, LANE), F32),
                   jax.ShapeDtypeStruct((bsz, SUBLANE, s_len), F32)],
        scratch_shapes=[pltpu.VMEM((1, LANE), F32)],
        compiler_params=_params(("parallel", "arbitrary"), est),
        name="small_gates",
    )(x_bf, w_small, bias_small, tri)


FOX_STRIP = 64
FOX_GROUP = 4
LOG2E = 1.4426950408889634


def _fox_kernel(q_ref, k_ref, v_ref, ct_ref, o_ref, vaug_ref, m_ref, acc_ref, p_ref, *, tq, scale):
    hp = pl.program_id(1)
    i = pl.program_id(2)
    D = FOX_HEAD_DIM
    grp = vaug_ref.shape[0]
    k1 = scale * LOG2E
    nstrip = tq // FOX_STRIP
    s_len = k_ref.shape[0]

    @pl.when(i == 0)
    def _():
        ones_col = (lax.broadcasted_iota(jnp.int32, (s_len, D), 1) == 0).astype(BF16)
        for hh in range(grp):
            vaug_ref[hh, :, 0:D] = v_ref[:, hh * D:(hh + 1) * D]
            vaug_ref[hh, :, D:2 * D] = ones_col

    def block(j, masked):
        ks = pl.multiple_of(j * tq, tq)
        for hh in range(grp):
            hs = slice(hh * D, (hh + 1) * D)
            c2 = ct_ref[0, pl.ds(hp * grp + hh, 1), pl.ds(ks, tq)] * LOG2E
            raw = _dot_nt(q_ref[:, hs], k_ref[pl.ds(ks, tq), hs])
            for r in range(nstrip):
                rs = slice(r * FOX_STRIP, (r + 1) * FOX_STRIP)
                cw = min(tq, -(-((r + 1) * FOX_STRIP) // D) * D) if masked else tq
                s = raw[rs, :cw] * k1 - c2[:, :cw]
                if masked:
                    row = lax.broadcasted_iota(jnp.int32, (FOX_STRIP, cw), 0) + r * FOX_STRIP
                    col = lax.broadcasted_iota(jnp.int32, (FOX_STRIP, cw), 1)
                    s = jnp.where(col <= row, s, -jnp.inf)
                m_new = jnp.broadcast_to(jnp.max(s, axis=1, keepdims=True), (FOX_STRIP, D))
                if not masked:
                    m_old = m_ref[hh, rs, :]
                    m_new = jnp.maximum(m_old, m_new)
                    corr = jnp.exp2(m_old - m_new)
                    acc_ref[hh, rs, :] = acc_ref[hh, rs, :] * jnp.concatenate([corr, corr], axis=1)
                p = jnp.exp2(s - jnp.concatenate([m_new] * (cw // D), axis=1))
                m_ref[hh, rs, :] = m_new
                p_ref[hh, rs, :cw] = p.astype(BF16)
                if cw < tq:
                    p_ref[hh, rs, cw:] = jnp.zeros((FOX_STRIP, tq - cw), BF16)
            pv = jnp.dot(p_ref[hh], vaug_ref[hh, pl.ds(ks, tq), :], preferred_element_type=F32)
            if masked:
                acc_ref[hh] = pv
            else:
                acc_ref[hh] += pv

    block(i, True)

    def body(j, carry):
        block(j, False)
        return carry

    lax.fori_loop(0, i, body, 0)
    for hh in range(grp):
        acc = acc_ref[hh]
        o_ref[:, hh * D:(hh + 1) * D] = (acc[:, 0:D] / acc[:, D:D + 1]).astype(o_ref.dtype)


def _fox_attention(u, ct, bsz, s_len, mix, n_heads):
    m = u.shape[0]
    tq = _pick_tile(s_len, 512, LANE)
    nq = s_len // tq
    grp = math.gcd(n_heads, FOX_GROUP)
    pw = grp * FOX_HEAD_DIM
    assert tq % FOX_STRIP == 0
    hb = mix // pw
    est = (2 * (2 * tq * pw * 2 + 2 * s_len * pw * 2 + SUBLANE * s_len * 4)
           + grp * (2 * s_len * FOX_HEAD_DIM * 2 + tq * LANE * 4 + tq * pw * 4 + tq * tq * 2) + 6 * tq * tq * 4)
    return pl.pallas_call(
        functools.partial(_fox_kernel, tq=tq, scale=FOX_HEAD_DIM ** -0.5),
        grid=(bsz, n_heads // grp, nq),
        in_specs=[pl.BlockSpec((tq, pw), lambda b, h, i: (b * nq + i, h)),
                  pl.BlockSpec((s_len, pw), lambda b, h, i: (b, hb + h)),
                  pl.BlockSpec((s_len, pw), lambda b, h, i: (b, 2 * hb + h)),
                  pl.BlockSpec((1, SUBLANE, s_len), lambda b, h, i: (b, 0, 0))],
        out_specs=pl.BlockSpec((tq, pw), lambda b, h, i: (b * nq + i, h)),
        out_shape=jax.ShapeDtypeStruct((m, mix), BF16),
        scratch_shapes=[pltpu.VMEM((grp, s_len, 2 * FOX_HEAD_DIM), BF16),
                        pltpu.VMEM((grp, tq, FOX_HEAD_DIM), F32),
                        pltpu.VMEM((grp, tq, 2 * FOX_HEAD_DIM), F32),
                        pltpu.VMEM((grp, tq, tq), BF16)],
        compiler_params=_params(("parallel", "parallel", "arbitrary"), est),
        name="fox_attention",
    )(u, u, u, ct)


def _ssd_kernel(z_ref, xbc_ref, aux_ref, cw_ref, cb_ref, aneg_ref, e_ref, tri_ref, d_ref, g_ref,
                *refs, mix, groups, kconv, n_side):
    side_in = refs[:n_side]
    o_ref = refs[n_side]
    side_out = refs[n_side + 1:2 * n_side + 1]
    xext_ref, state_ref = refs[2 * n_side + 1:]
    for s_in, s_out in zip(side_in, side_out):
        s_out[...] = s_in[...].astype(BF16)
    L = SSD_CHUNK
    N = SSD_STATE
    GW = SSD_GROUP_WIDTH
    c = pl.program_id(1)

    @pl.when(c == 0)
    def _():
        state_ref[...] = jnp.zeros_like(state_ref)
        xext_ref[0:SUBLANE, :] = jnp.zeros((SUBLANE, xext_ref.shape[1]), F32)

    cur = xbc_ref[...]
    prev8 = xext_ref[0:SUBLANE, :]
    row8 = lax.broadcasted_iota(jnp.int32, prev8.shape, 0)
    conv = cb_ref[...] + cw_ref[kconv - 1:kconv, :] * cur
    for dshift in range(1, kconv):
        rolled = pltpu.roll(cur, dshift, axis=0)
        head = jnp.where(row8 < dshift, pltpu.roll(prev8, dshift, axis=0), rolled[0:SUBLANE])
        shifted = jnp.concatenate([head, rolled[SUBLANE:]], axis=0)
        conv = conv + cw_ref[kconv - 1 - dshift:kconv - dshift, :] * shifted
    xext_ref[0:SUBLANE, :] = cur[L - SUBLANE:L, :]
    xbc = conv * _sigmoid(conv)
    xs = xbc[:, :mix]

    aux = aux_ref[...]
    dta = aux * aneg_ref[...]
    a_all = _dot_sel_lhs(tri_ref[...], dta)
    a_all_t = a_all.T
    e = e_ref[...]
    dt_exp = _dot_sel_rhs(aux, e)
    a_exp = _dot_sel_rhs(a_all, e)
    a_last = a_exp[L - 1:L, :]
    ea = jnp.exp(a_exp)
    xc = xs * dt_exp
    xd = (xc * jnp.exp(a_last - a_exp)).astype(BF16)

    row = lax.broadcasted_iota(jnp.int32, (L, L), 0)
    col = lax.broadcasted_iota(jnp.int32, (L, L), 1)
    causal = col <= row
    lane_lo = lax.broadcasted_iota(jnp.int32, (L, LANE), 1) < SSD_HEAD_DIM

    y_parts = []
    for g in range(groups):
        bg = xbc[:, mix + g * N: mix + (g + 1) * N].astype(BF16)
        cg = xbc[:, mix + (groups + g) * N: mix + (groups + g + 1) * N].astype(BF16)
        cbm = _dot_nt(cg, bg)
        st = state_ref[g]
        gsl = slice(g * GW, (g + 1) * GW)
        y_off = jnp.dot(cg, st.astype(BF16), preferred_element_type=F32) * ea[:, gsl]
        state_ref[g] = ea[L - 1:L, gsl] * st + _dot_tn(bg, xd[:, gsl])
        y_cols = []
        for pair in range(GW // LANE):
            c0 = g * GW + pair * LANE
            xcp = xc[:, c0:c0 + LANE]
            acc = None
            for half in range(LANE // SSD_HEAD_DIM):
                hh = DT_COL + c0 // SSD_HEAD_DIM + half
                seg = a_all[:, hh:hh + 1] - a_all_t[hh:hh + 1, :]
                dec = jnp.where(causal, jnp.exp(jnp.minimum(seg, 0.0)), 0.0)
                gm = (cbm * dec).astype(BF16)
                xm = jnp.where(lane_lo if half == 0 else jnp.logical_not(lane_lo), xcp, 0.0).astype(BF16)
                part = jnp.dot(gm, xm, preferred_element_type=F32)
                acc = part if acc is None else acc + part
            y_cols.append(acc)
        y_diag = jnp.concatenate(y_cols, axis=1)
        y = y_diag + y_off + d_ref[:, gsl] * xs[:, gsl]
        zg = z_ref[:, gsl]
        yz = y * (zg * _sigmoid(zg))
        ms = jnp.mean(yz * yz, axis=1, keepdims=True)
        y_parts.append(yz * lax.rsqrt(ms + RMS_EPS) * g_ref[:, gsl])
    o_ref[...] = jnp.concatenate(y_parts, axis=1).astype(o_ref.dtype)


def _ssd_mixer(u, z_col, xbc_col, aux, conv_w, conv_b, a_neg_row, d_exp, norm_g, bsz, s_len, mix, n_heads, sides):
    m = u.shape[0]
    L = SSD_CHUNK
    groups = mix // SSD_GROUP_WIDTH
    width = mix + 2 * groups * SSD_STATE
    kconv = conv_w.shape[0]
    nchunk = s_len // L
    assert z_col % mix == 0 and xbc_col % width == 0
    z_blk = z_col // mix
    xbc_blk = xbc_col // width
    expand = np.zeros((LANE, mix), np.float32)
    for hh in range(n_heads):
        expand[DT_COL + hh, hh * SSD_HEAD_DIM:(hh + 1) * SSD_HEAD_DIM] = 1.0
    tri = np.tril(np.ones((L, L), np.float32))
    est = 2 * (L * mix * 4 + L * width * 4 + L * LANE * 4 + LANE * mix * 2 + L * mix * 2) + 24 * L * width * 4
    plans = [_side_plan(side, (bsz, nchunk)) for side in sides]
    est += sum(2 * p[4] * 6 for p in plans)
    res = pl.pallas_call(
        functools.partial(_ssd_kernel, mix=mix, groups=groups, kconv=kconv, n_side=len(sides)),
        grid=(bsz, nchunk),
        in_specs=[pl.BlockSpec((L, mix), lambda b, c: (b * nchunk + c, z_blk)),
                  pl.BlockSpec((L, width), lambda b, c: (b * nchunk + c, xbc_blk)),
                  pl.BlockSpec((L, LANE), lambda b, c: (b * nchunk + c, 0)),
                  pl.BlockSpec((kconv, width), lambda b, c: (0, 0)),
                  pl.BlockSpec((1, width), lambda b, c: (0, 0)),
                  pl.BlockSpec((1, LANE), lambda b, c: (0, 0)),
                  pl.BlockSpec((LANE, mix), lambda b, c: (0, 0)),
                  pl.BlockSpec((L, L), lambda b, c: (0, 0)),
                  pl.BlockSpec((1, mix), lambda b, c: (0, 0)),
                  pl.BlockSpec((1, mix), lambda b, c: (0, 0))] + [p[1] for p in plans],
        out_specs=[pl.BlockSpec((L, mix), lambda b, c: (b * nchunk + c, 0))] + [p[2] for p in plans],
        out_shape=[jax.ShapeDtypeStruct((m, mix), BF16)] + [p[3] for p in plans],
        scratch_shapes=[pltpu.VMEM((L + SUBLANE, width), F32),
                        pltpu.VMEM((groups, SSD_STATE, SSD_GROUP_WIDTH), F32)],
        compiler_params=_params(("parallel", "arbitrary"), est),
        name="ssd_mixer",
    )(u, u, aux, conv_w, conv_b.reshape(1, width), a_neg_row, jnp.asarray(expand, BF16),
      jnp.asarray(tri, BF16), d_exp.reshape(1, mix), norm_g.reshape(1, mix), *[p[0] for p in plans])
    return res[0], [r.reshape(p[5]) for r, p in zip(res[1:], plans)]


def _hgrn_tables(blk):
    nlev = int(np.log2(blk))
    tri = np.tril(np.ones((blk, blk), np.float32))
    rows = []
    t = np.arange(blk)
    for l in range(1, HGRN_MM_LEVELS + 1):
        size = 1 << l
        mid = (t // size) * size + size // 2 - 1
        sign = np.where(t > mid, 1.0, -1.0).astype(np.float32)[:, None]
        rows.append(sign * (tri - tri[mid]))
    wlev = np.concatenate(rows, axis=0)
    th = np.arange(blk // 2)
    x = th[:, None] ^ th[None, :]
    lv = np.where(x > 0, np.floor(np.log2(np.maximum(x, 1))).astype(np.int32) + 1, 0)
    lv = np.where(th[None, :] > th[:, None], -1, lv).astype(np.int32)
    return tri, wlev, lv, nlev


def _hgrn_kernel(q_ref, f_ref, i_ref, g_ref, lb_ref, ng_ref, tri_ref, wlev_ref, lv_ref, o_ref, state_ref,
                 *, blk, nlev, n_heads, grp):
    c = pl.program_id(1)

    @pl.when(c == 0)
    def _():
        state_ref[...] = jnp.zeros_like(state_ref)

    half = blk // 2
    top = slice(0, half)
    bot = slice(half, blk)
    pw = grp * HGRN_HEAD_DIM

    def head_pair(p, carry):
        sl = pl.ds(pl.multiple_of(p * pw, pw), pw)
        hq = q_ref[:, sl]
        lb = lb_ref[:, sl]
        f = lb + (1.0 - lb) * _sigmoid(f_ref[:, sl])
        g2 = jnp.log(f) * LOG2E
        k = 1.0 - f
        q = hq * _sigmoid(hq)
        hi, mid, lo = _split3(g2)
        dot = lambda w, x: jnp.dot(w, x, preferred_element_type=F32)
        tri = tri_ref[...]
        b = dot(tri, hi) + (dot(tri, mid) + dot(tri, lo))
        wlev = wlev_ref[...]
        dlev = dot(wlev, hi) + dot(wlev, mid)
        facs = []
        for l in range(1, nlev + 1):
            if l <= HGRN_MM_LEVELS:
                dneg = dlev[(l - 1) * blk:l * blk]
            else:
                size = 1 << l
                hs2 = size // 2
                pieces = []
                for r0 in range(0, blk, size):
                    bmid = b[r0 + hs2 - 1:r0 + hs2]
                    pieces.append(bmid - b[r0:r0 + hs2])
                    pieces.append(b[r0 + hs2:r0 + size] - bmid)
                dneg = jnp.concatenate(pieces, axis=0)
            facs.append(jnp.exp2(dneg).astype(BF16))
        eb = jnp.exp2(b)
        b_last = b[blk - 1:blk, :]
        e_last = jnp.exp2(b_last)
        kdec = (k * jnp.exp2(b_last - b)).astype(BF16)
        qe = (q * eb).astype(BF16)
        qb2 = q.astype(BF16)
        kb2 = k.astype(BF16)
        hg = g_ref[:, sl]
        gate = ng_ref[:, sl] * (hg * _sigmoid(hg))
        lv = lv_ref[...]
        for hh in range(grp):
            hs = slice(hh * HGRN_HEAD_DIM, (hh + 1) * HGRN_HEAD_DIM)
            hidx = p * grp + hh
            osl = pl.ds(pl.multiple_of(hidx * HGRN_HEAD_DIM, HGRN_HEAD_DIM), HGRN_HEAD_DIM)
            qb = qb2[:, hs]
            kb = kb2[:, hs]
            vb = i_ref[:, osl]
            a_tl = jnp.where(lv == 0, _dot_nt(qb[top], kb[top]), 0.0)
            a_br = jnp.where(lv == 0, _dot_nt(qb[bot], kb[bot]), 0.0)
            for l in range(1, nlev):
                fl = facs[l - 1][:, hs]
                qf = qb * fl
                kf = kb * fl
                a_tl = jnp.where(lv == l, _dot_nt(qf[top], kf[top]), a_tl)
                a_br = jnp.where(lv == l, _dot_nt(qf[bot], kf[bot]), a_br)
            ft = facs[nlev - 1][:, hs]
            a_bl = _dot_nt(qb[bot] * ft[bot], kb[top] * ft[top])
            st = state_ref[hidx]
            o_top = jnp.dot(a_tl.astype(BF16), vb[top], preferred_element_type=F32)
            o_bot = (jnp.dot(a_bl.astype(BF16), vb[top], preferred_element_type=F32)
                     + jnp.dot(a_br.astype(BF16), vb[bot], preferred_element_type=F32))
            o = jnp.concatenate([o_top, o_bot], axis=0)
            o = o + _dot_nt(qe[:, hs], st.astype(BF16))
            state_ref[hidx] = e_last[:, hs] * st + _dot_tn(vb, kdec[:, hs])
            ms = jnp.mean(o * o, axis=1, keepdims=True)
            y = o * lax.rsqrt(ms + RMS_EPS) * gate[:, hs]
            o_ref[:, osl] = y.astype(o_ref.dtype)
        return carry

    lax.fori_loop(0, n_heads // grp, head_pair, 0)


def _hgrn_mixer(uf, q_col, f_col, g_col, ub, i_col, lb, norm_g, bsz, s_len, mix):
    m = uf.shape[0]
    blk = _pick_tile(s_len, HGRN_BLOCK, LANE)
    n_heads = mix // HGRN_HEAD_DIM
    tri, wlev, lv, nlev = _hgrn_tables(blk)
    assert nlev > HGRN_MM_LEVELS
    grp = math.gcd(n_heads, HGRN_GROUP)
    nblk = s_len // blk
    spec = lambda col: pl.BlockSpec((blk, mix), lambda b, c, k=col // mix: (b * nblk + c, k))
    vec = pl.BlockSpec((1, mix), lambda b, c: (0, 0))
    pw = grp * HGRN_HEAD_DIM
    est = (2 * (3 * blk * mix * 4 + 2 * blk * mix * 2 + (tri.size + wlev.size) * 2 + lv.size * 4)
           + (2 * wlev.shape[0] + (nlev + 16) * blk) * pw * 4 + 12 * blk * blk * 4)
    return pl.pallas_call(
        functools.partial(_hgrn_kernel, blk=blk, nlev=nlev, n_heads=n_heads, grp=grp),
        grid=(bsz, nblk),
        in_specs=[spec(q_col), spec(f_col), spec(i_col), spec(g_col), vec, vec,
                  pl.BlockSpec(tri.shape, lambda b, c: (0, 0)),
                  pl.BlockSpec(wlev.shape, lambda b, c: (0, 0)),
                  pl.BlockSpec(lv.shape, lambda b, c: (0, 0))],
        out_specs=pl.BlockSpec((blk, mix), lambda b, c: (b * nblk + c, 0)),
        out_shape=jax.ShapeDtypeStruct((m, mix), BF16),
        scratch_shapes=[pltpu.VMEM((n_heads, HGRN_HEAD_DIM, HGRN_HEAD_DIM), F32)],
        compiler_params=_params(("parallel", "arbitrary"), est),
        name="hgrn2_mixer",
    )(uf, uf, ub, uf, lb.reshape(1, mix), norm_g.reshape(1, mix), jnp.asarray(tri, BF16),
      jnp.asarray(wlev, BF16), jnp.asarray(lv))


def _merge_kernel(gl_ref, ya_ref, yb_ref, yc_ref, wg_ref, bg_ref, wb_ref, side_ref, o_ref, side_out_ref):
    gl = gl_ref[...]
    acc = None
    for i, y_ref in enumerate((ya_ref, yb_ref, yc_ref)):
        gate = _sigmoid(jnp.dot(gl, wg_ref[i], preferred_element_type=F32) + bg_ref[i])
        contrib = gate * jnp.dot(y_ref[...], wb_ref[i], preferred_element_type=F32)
        acc = contrib if acc is None else acc + contrib
    o_ref[...] = acc.astype(o_ref.dtype)
    side_out_ref[...] = side_ref[...].astype(BF16)


def _gated_merge(u, gate_col, rank, ya, yb, yc, w_gate, b_gate, w_branch, side):
    m = u.shape[0]
    mix = ya.shape[1]
    nbr, _, d = w_gate.shape
    tm = _pick_tile(m, 1024)
    tn = _pick_tile(d, 512)
    grid = (m // tm, d // tn)
    ybs = pl.BlockSpec((tm, mix), lambda i, j: (i, 0))
    view, s_in, s_out, sshape, selems, side_shape = _side_plan(side, grid)
    est = (2 * (tm * rank * 2 + 3 * tm * mix * 2 + nbr * rank * tn * 2 + nbr * mix * tn * 2 + tm * tn * 2 + selems * 6)
           + 6 * tm * tn * 4)
    tile = pl.BlockSpec((tm, tn), lambda i, j: (i, j))
    merged, side_bf = pl.pallas_call(
        _merge_kernel,
        grid=grid,
        in_specs=[pl.BlockSpec((tm, rank), lambda i, j: (i, gate_col // rank)),
                  ybs, ybs, ybs,
                  pl.BlockSpec((nbr, rank, tn), lambda i, j: (0, 0, j)),
                  pl.BlockSpec((nbr, 1, tn), lambda i, j: (0, 0, j)),
                  pl.BlockSpec((nbr, mix, tn), lambda i, j: (0, 0, j)),
                  s_in],
        out_specs=[tile, s_out],
        out_shape=[jax.ShapeDtypeStruct((m, d), BF16), sshape],
        compiler_params=_params(("parallel", "parallel"), est),
        name="gated_merge",
    )(u, ya, yb, yc, w_gate, b_gate.reshape(nbr, 1, d), w_branch, view)
    return merged, side_bf.reshape(side_shape)


def kernel(x, w_in, fox_f_bias, ssd_conv_w, ssd_conv_b, ssd_dt_bias, ssd_a_log, ssd_d, ssd_norm_g, hgrn_lower_bound, hgrn_norm_g, w_gate, b_gate, w_branch, w_out, ln1_g, ln1_b, w_up, w_down, ln2_g, ln2_b):
    bsz, s_len, d = x.shape
    depth = w_in.shape[0]
    mix = d // 4
    n_fox = fox_f_bias.shape[1]
    n_ssd = ssd_dt_bias.shape[1]
    rank = w_gate.shape[2]
    conv_dim = ssd_conv_w.shape[2]
    alpha = (2 * depth) ** 0.25
    assert mix % SSD_GROUP_WIDTH == 0 and conv_dim == 2 * mix and n_fox <= DT_COL and DT_COL + n_ssd <= LANE
    assert n_fox * FOX_HEAD_DIM == mix and n_ssd * SSD_HEAD_DIM == mix
    assert s_len % SSD_CHUNK == 0 and (4 * mix) % rank == 0

    widths = (mix, mix, mix, n_fox, mix, conv_dim, n_ssd, mix, mix, mix, mix, rank)
    offs = np.concatenate([[0], np.cumsum(widths)])
    col = lambda k: slice(int(offs[k]), int(offs[k + 1]))
    bf_groups = (0, 1, 2, 9, 11)
    f32_groups = (5, 4, 7, 8, 10)
    hi_col, gate_col = 3 * mix, 4 * mix
    xbc_col, z_col, hq_col, hf_col, hg_col = 0, conv_dim, conv_dim + mix, conv_dim + 2 * mix, conv_dim + 3 * mix

    lb_all = jnp.cumsum(jax.nn.softmax(hgrn_lower_bound.astype(F32), axis=0), axis=0)
    lb_all = lb_all - lb_all[0]

    m = bsz * s_len
    xf = x.reshape(m, d).astype(F32)
    xb = xf.astype(BF16)
    identity_stats = jnp.zeros((m, LANE), F32).at[:, 1].set(1.0)
    res = (xf, identity_stats, jnp.ones((d,), F32), jnp.zeros((d,), F32))
    pad = LANE - DT_COL - n_ssd
    piece = lambda k: (int(offs[k]), int(widths[k]))
    small_plan = [piece(3)] + ([(None, DT_COL - n_fox)] if n_fox < DT_COL else []) + [piece(6)] + ([(None, pad)] if pad else [])
    w_bf_all, w_f32_all, w_small_all = _regroup_weights(
        w_in.astype(F32), [[piece(k) for k in bf_groups], [piece(k) for k in f32_groups], small_plan])
    for l in range(depth):
        bias_small = jnp.concatenate(
            [fox_f_bias[l], jnp.zeros((DT_COL - n_fox,), F32), ssd_dt_bias[l], jnp.zeros((pad,), F32)]).reshape(1, LANE)
        a_neg_row = jnp.concatenate(
            [jnp.zeros((DT_COL,), F32), -jnp.exp(ssd_a_log[l].astype(F32)), jnp.zeros((pad,), F32)]).reshape(1, LANE)

        ub = _matmul(xb, w_bf_all, BF16, layer=l, tn_cap=768)
        uf = _matmul(xb, w_f32_all, F32, layer=l)
        aux, ct = _small_gates(xb, w_small_all, l, bias_small, bsz, s_len, n_fox)
        y_a = _fox_attention(ub, ct, bsz, s_len, mix, n_fox)
        y_b, (w_gate_bf, w_branch_bf, w_out_bf) = _ssd_mixer(
            uf, z_col, xbc_col, aux, ssd_conv_w[l], ssd_conv_b[l], a_neg_row,
            jnp.repeat(ssd_d[l], SSD_HEAD_DIM), ssd_norm_g[l], bsz, s_len, mix, n_ssd,
            [(w_gate, l), (w_branch, l), (w_out, l)])
        y_c = _hgrn_mixer(uf, hq_col, hf_col, hg_col, ub, hi_col, lb_all[l],
                          jnp.tile(hgrn_norm_g[l], mix // HGRN_HEAD_DIM), bsz, s_len, mix)
        merged, w_up_bf = _gated_merge(ub, gate_col, rank, y_a, y_b, y_c,
                                       w_gate_bf, b_gate[l], w_branch_bf, (w_up, l))
        z1 = _matmul(merged, w_out_bf, F32, residual=res, alpha=alpha)
        xb, st1 = _layernorm(z1, ln1_g[l], ln1_b[l])
        res = (z1, st1, ln1_g[l].astype(F32), ln1_b[l].astype(F32))
        hmid, w_down_bf = _matmul(xb, w_up_bf, BF16, act="relu2", side=(w_down, l))
        z2 = _matmul_ktiled_residual(hmid, w_down_bf, res, alpha)
        if l == depth - 1:
            out = _layernorm(z2, ln2_g[l], ln2_b[l], final=True)
        else:
            xb, st2 = _layernorm(z2, ln2_g[l], ln2_b[l])
            res = (z2, st2, ln2_g[l].astype(F32), ln2_b[l].astype(F32))
    return out.reshape(bsz, s_len, d).astype(x.dtype)
```
